```python
import math
import jax, jax.numpy as jnp
from jax import lax
import numpy as np

D_MODEL = 1024
BATCH = 2
SEQ = 8192
DEPTH = 2
DEC_BATCH = 4
DEC_SEQ = 4096
PAST_LEN = 128

N_META = 16
GRID_W = 64
MIX_WIDTH = D_MODEL
POOL_WIDTH = MIX_WIDTH // 2
POOL_WINDOWS = (2, 4, 8, 16)
N_POOL_GROUPS = len(POOL_WINDOWS)
POOL_GROUP = POOL_WIDTH // N_POOL_GROUPS
HEAD_DIM = 64
N_HEADS = (MIX_WIDTH - POOL_WIDTH) // HEAD_DIM
N_KV_HEADS = 2
Q_PER_KV = N_HEADS // N_KV_HEADS
ATTN_WIDTH = N_HEADS * HEAD_DIM
KV_WIDTH = N_KV_HEADS * HEAD_DIM
IN_WIDTH = POOL_WIDTH + ATTN_WIDTH + 2 * KV_WIDTH
ROT_PAIRS = HEAD_DIM // 4
ROPE_THETA = 10000.0
Q_BLOCK = 128
D_FF = 4 * D_MODEL
EPS = 1e-6

kernel_name = "hybrid_pool_gqa_encoder"


def rmsnorm(x, g):
    xf = x.astype(jnp.float32)
    y = xf * lax.rsqrt(jnp.mean(xf * xf, axis=-1, keepdims=True) + EPS)
    return (y * g.astype(jnp.float32)).astype(x.dtype)


def rope_tables(n_tok):
    rows = n_tok // GRID_W
    r = jnp.concatenate([jnp.zeros((N_META,), jnp.int32),
                         jnp.repeat(jnp.arange(rows, dtype=jnp.int32), GRID_W)]).astype(jnp.float32)
    c = jnp.concatenate([jnp.zeros((N_META,), jnp.int32),
                         jnp.tile(jnp.arange(GRID_W, dtype=jnp.int32), rows)]).astype(jnp.float32)
    inv = ROPE_THETA ** (-jnp.arange(ROT_PAIRS, dtype=jnp.float32) / ROT_PAIRS)
    ang = jnp.stack([r[:, None] * inv, c[:, None] * inv], axis=1)
    return jnp.cos(ang), jnp.sin(ang)


def apply_rope(x, cos, sin):
    B, L, H, _ = x.shape
    xf = x.astype(jnp.float32).reshape(B, L, H, 2, 2, ROT_PAIRS)
    x1, x2 = xf[..., 0, :], xf[..., 1, :]
    c = cos[None, :, None]
    s = sin[None, :, None]
    out = jnp.stack([x1 * c - x2 * s, x2 * c + x1 * s], axis=-2)
    return out.reshape(B, L, H, HEAD_DIM).astype(x.dtype)


def pool_mixer(u, w_pool, pool_scale):
    B, L, _ = u.shape
    uf = u.astype(jnp.float32)
    cs = jnp.concatenate([jnp.zeros((B, 1, POOL_WIDTH), jnp.float32), jnp.cumsum(uf, axis=1)], axis=1)
    csg = cs.reshape(B, L + 1, N_POOL_GROUPS, POOL_GROUP)
    ug = uf.reshape(B, L, N_POOL_GROUPS, POOL_GROUP)
    t = jnp.arange(L, dtype=jnp.int32)
    outs = []
    for gi, w in enumerate(POOL_WINDOWS):
        lo = jnp.clip(t - w // 2, 0, L)
        hi = jnp.clip(t + (w - w // 2), 0, L)
        cnt = (hi - lo).astype(jnp.float32)[None, :, None]
        mean = (csg[:, hi, gi] - csg[:, lo, gi]) / cnt
        outs.append(mean - ug[:, :, gi])
    d = jnp.stack(outs, axis=2).astype(u.dtype)
    y = jnp.einsum('blgc,gce->blge', d, w_pool).reshape(B, L, POOL_WIDTH)
    return y * pool_scale


def block_attention(q, k, v):
    B, L = q.shape[0], q.shape[1]
    scale = 1.0 / math.sqrt(HEAD_DIM)

    def attend(qb):
        s = jnp.einsum('bqkgd,bskd->bkgqs', qb, k, preferred_element_type=jnp.float32) * scale
        p = jax.nn.softmax(s, axis=-1).astype(v.dtype)
        return jnp.einsum('bkgqs,bskd->bqkgd', p, v)

    o_meta = attend(q[:, :N_META])
    n_tok = L - N_META
    n_blk = n_tok // Q_BLOCK
    qr = q[:, N_META:].reshape(B, n_blk, Q_BLOCK, N_KV_HEADS, Q_PER_KV, HEAD_DIM)
    o_real = lax.map(attend, jnp.moveaxis(qr, 1, 0))
    o_real = jnp.moveaxis(o_real, 0, 1).reshape(B, n_tok, N_KV_HEADS, Q_PER_KV, HEAD_DIM)
    return jnp.concatenate([o_meta, o_real], axis=1)


def encoder_layer(h, cos, sin, norm1_g, w_in, q_norm_g, k_norm_g, w_pool, pool_scale,
                  w_out, norm2_g, w_mlp_in, w_mlp_out):
    B, L, _ = h.shape
    n = rmsnorm(h, norm1_g)
    proj = n @ w_in
    u_pool = proj[..., :POOL_WIDTH]
    q = proj[..., POOL_WIDTH:POOL_WIDTH + ATTN_WIDTH].reshape(B, L, N_HEADS, HEAD_DIM)
    k = proj[..., POOL_WIDTH + ATTN_WIDTH:POOL_WIDTH + ATTN_WIDTH + KV_WIDTH].reshape(B, L, N_KV_HEADS, HEAD_DIM)
    v = proj[..., POOL_WIDTH + ATTN_WIDTH + KV_WIDTH:].reshape(B, L, N_KV_HEADS, HEAD_DIM)
    pool_out = pool_mixer(u_pool, w_pool, pool_scale)
    q = apply_rope(rmsnorm(q, q_norm_g), cos, sin).reshape(B, L, N_KV_HEADS, Q_PER_KV, HEAD_DIM)
    k = apply_rope(rmsnorm(k, k_norm_g), cos, sin)
    attn_out = block_attention(q, k, v).reshape(B, L, ATTN_WIDTH)
    h = h + jnp.concatenate([pool_out, attn_out], axis=-1) @ w_out
    m = rmsnorm(h, norm2_g) @ w_mlp_in
    h = h + jnp.square(jax.nn.relu(m)) @ w_mlp_out
    return h


def run_trunk(x, meta_tokens, norm1_g, w_in, q_norm_g, k_norm_g, w_pool, pool_scale,
              w_out, norm2_g, w_mlp_in, w_mlp_out):
    B, n_tok, _ = x.shape
    cos, sin = rope_tables(n_tok)
    meta = jnp.broadcast_to(meta_tokens[None].astype(x.dtype), (B, N_META, D_MODEL))
    h = jnp.concatenate([meta, x], axis=1)
    for l in range(DEPTH):
        h = encoder_layer(h, cos, sin, norm1_g[l], w_in[l], q_norm_g[l], k_norm_g[l], w_pool[l],
                          pool_scale[l], w_out[l], norm2_g[l], w_mlp_in[l], w_mlp_out[l])
    return h[:, N_META:]


def setup_inputs(seed: int = 0) -> dict:
    key = jax.random.key(seed)
    ks = jax.random.split(key, 14)
    f32 = jnp.float32
    nrm = lambda k, shape, s: jax.random.normal(k, shape, f32) * s
    return {
        "x_prompt": nrm(ks[0], (BATCH, SEQ, D_MODEL), 1.0),
        "x_sample": nrm(ks[1], (DEC_BATCH, DEC_SEQ, D_MODEL), 1.0),
        "meta_tokens": nrm(ks[2], (N_META, D_MODEL), 1.0),
        "norm1_g": 1.0 + nrm(ks[3], (DEPTH, D_MODEL), 0.02),
        "w_in": nrm(ks[4], (DEPTH, D_MODEL, IN_WIDTH), D_MODEL ** -0.5),
        "q_norm_g": 1.0 + nrm(ks[5], (DEPTH, HEAD_DIM), 0.02),
        "k_norm_g": 1.0 + nrm(ks[6], (DEPTH, HEAD_DIM), 0.02),
        "w_pool": nrm(ks[7], (DEPTH, N_POOL_GROUPS, POOL_GROUP, POOL_GROUP), POOL_GROUP ** -0.5),
        "pool_scale": 1.0 + nrm(ks[8], (DEPTH, POOL_WIDTH), 0.1),
        "w_out": nrm(ks[9], (DEPTH, MIX_WIDTH, D_MODEL), MIX_WIDTH ** -0.5),
        "norm2_g": 1.0 + nrm(ks[10], (DEPTH, D_MODEL), 0.02),
        "w_mlp_in": nrm(ks[11], (DEPTH, D_MODEL, D_FF), D_MODEL ** -0.5),
        "w_mlp_out": nrm(ks[12], (DEPTH, D_FF, D_MODEL), 0.5 * D_FF ** -0.5),
    }


def reference(x_prompt, x_sample, meta_tokens, norm1_g, w_in, q_norm_g, k_norm_g, w_pool,
              pool_scale, w_out, norm2_g, w_mlp_in, w_mlp_out):
    y_prompt = run_trunk(x_prompt, meta_tokens, norm1_g, w_in, q_norm_g, k_norm_g, w_pool,
                         pool_scale, w_out, norm2_g, w_mlp_in, w_mlp_out)
    y_sample = run_trunk(x_sample, meta_tokens, norm1_g, w_in, q_norm_g, k_norm_g, w_pool,
                         pool_scale, w_out, norm2_g, w_mlp_in, w_mlp_out)
    return (y_prompt, y_sample)
```

```python
import functools
import math

import jax
import jax.numpy as jnp
from jax import lax
from jax.experimental import pallas as pl
from jax.experimental.pallas import tpu as pltpu

D_MODEL = 1024
N_META = 16
GRID_W = 64
POOL_WIDTH = 512
POOL_WINDOWS = (2, 4, 8, 16)
POOL_GROUP = 128
HEAD_DIM = 64
N_HEADS = 8
N_KV_HEADS = 2
Q_PER_KV = N_HEADS // N_KV_HEADS
ATTN_WIDTH = N_HEADS * HEAD_DIM
KV_WIDTH = N_KV_HEADS * HEAD_DIM
IN_WIDTH = POOL_WIDTH + ATTN_WIDTH + 2 * KV_WIDTH
ROT_PAIRS = HEAD_DIM // 4
ROPE_THETA = 10000.0
D_FF = 4 * D_MODEL
EPS = 1e-6

LANES = 128
SUBLANES = 8
PAD = LANES
HALO = SUBLANES
VMEM_LIMIT = 56 * 1024 * 1024

F32 = jnp.float32
BF16 = jnp.bfloat16


def _pick_block(n, cap):
    best = LANES
    for t in range(LANES, cap + 1, LANES):
        if n % t == 0:
            best = t
    return best


def _plan(lp):
    return _pick_block(lp, 768), LANES, _pick_block(lp, 768)


def _const_spec(shape):
    nd = len(shape)
    return pl.BlockSpec(shape, lambda *_: (0,) * nd, pipeline_mode=pl.Buffered(1))


def _inproj_kernel(h_ref, g1_ref, win_ref, qg_ref, kg_ref, bd_ref, cos_ref, sa_ref, sb_ref,
                   u_ref, q_ref, kt_ref, v_ref):
    x = h_ref[...]
    ms = jnp.mean(x * x, axis=-1, keepdims=True)
    n = (x * lax.rsqrt(ms + EPS) * g1_ref[...]).astype(BF16)
    proj = jnp.dot(n, win_ref[...], preferred_element_type=F32)
    u_ref[...] = proj[:, :POOL_WIDTH]

    cos, sa, sb, bd = cos_ref[...], sa_ref[...], sb_ref[...], bd_ref[...]

    def norm_rope(xc, g):
        sq = xc * xc
        hi = sq.astype(BF16)
        lo = (sq - hi.astype(F32)).astype(BF16)
        ss = (jnp.dot(hi, bd, preferred_element_type=F32)
              + jnp.dot(lo, bd, preferred_element_type=F32))
        y = xc * lax.rsqrt(ss * (1.0 / HEAD_DIM) + EPS) * g
        return (y * cos + pltpu.roll(y, LANES - ROT_PAIRS, 1) * sa
                + pltpu.roll(y, ROT_PAIRS, 1) * sb)

    scale = 1.0 / math.sqrt(HEAD_DIM)
    for c in range(ATTN_WIDTH // LANES):
        lo_col = POOL_WIDTH + c * LANES
        qc = norm_rope(proj[:, lo_col:lo_col + LANES], qg_ref[...]) * scale
        q_ref[2 * c] = qc[:, :HEAD_DIM].astype(BF16)
        q_ref[2 * c + 1] = qc[:, HEAD_DIM:].astype(BF16)

    k0 = POOL_WIDTH + ATTN_WIDTH
    kc = norm_rope(proj[:, k0:k0 + KV_WIDTH], kg_ref[...])
    kt_ref[...] = kc.T.astype(BF16)

    vv = proj[:, k0 + KV_WIDTH:]
    lane = lax.broadcasted_iota(jnp.int32, vv.shape, 1)
    ones_col = jnp.where(lane == HEAD_DIM, 1.0, 0.0)
    v_ref[0] = jnp.where(lane < HEAD_DIM, vv, ones_col).astype(BF16)
    v_ref[1] = jnp.where(lane < HEAD_DIM, pltpu.roll(vv, HEAD_DIM, 1), ones_col).astype(BF16)


def _inproj(h, g1, w_in, qg, kg, bd, cos, sa, sb, t_rows):
    b, lp, d = h.shape
    nblk = lp // t_rows
    row_spec = lambda w: pl.BlockSpec((None, t_rows, w), lambda bi, i: (bi, i, 0))
    tab_spec = pl.BlockSpec((t_rows, LANES), lambda bi, i: (i, 0))
    return pl.pallas_call(
        _inproj_kernel,
        grid=(b, nblk),
        in_specs=[
            row_spec(d),
            _const_spec((1, d)),
            _const_spec((d, IN_WIDTH)),
            _const_spec((1, LANES)),
            _const_spec((1, LANES)),
            _const_spec((LANES, LANES)),
            tab_spec, tab_spec, tab_spec,
        ],
        out_specs=[
            row_spec(POOL_WIDTH),
            pl.BlockSpec((None, N_HEADS, t_rows, HEAD_DIM), lambda bi, i: (bi, 0, i, 0)),
            pl.BlockSpec((None, KV_WIDTH, t_rows), lambda bi, i: (bi, 0, i)),
            pl.BlockSpec((None, N_KV_HEADS, t_rows, LANES), lambda bi, i: (bi, 0, i, 0)),
        ],
        out_shape=[
            jax.ShapeDtypeStruct((b, lp, POOL_WIDTH), F32),
            jax.ShapeDtypeStruct((b, N_HEADS, lp, HEAD_DIM), BF16),
            jax.ShapeDtypeStruct((b, KV_WIDTH, lp), BF16),
            jax.ShapeDtypeStruct((b, N_KV_HEADS, lp, LANES), BF16),
        ],
        compiler_params=pltpu.CompilerParams(
            dimension_semantics=("arbitrary", "arbitrary"), vmem_limit_bytes=VMEM_LIMIT),
        name="inproj",
    )(h, g1, w_in, qg, kg, bd, cos, sa, sb)


def _attn_kernel(q_ref, kt_ref, v_ref, o_ref, *, tk, n_masked):
    tq = q_ref.shape[1]
    nk = kt_ref.shape[1] // tk
    qs = q_ref[...].reshape(Q_PER_KV * tq, HEAD_DIM)

    def scores(start):
        return jnp.dot(qs, kt_ref[:, pl.ds(start, tk)], preferred_element_type=F32)

    s = scores(0)
    col = lax.broadcasted_iota(jnp.int32, s.shape, 1)
    s = jnp.where(col >= n_masked, s, -jnp.inf)
    m = jnp.max(s, axis=1, keepdims=True)
    p = jnp.exp(s - m).astype(BF16)
    acc = jnp.dot(p, v_ref[0:tk, :], preferred_element_type=F32)

    def body(j, carry):
        m, acc = carry
        start = pl.multiple_of(j * tk, tk)
        s = scores(start)
        m_new = jnp.maximum(m, jnp.max(s, axis=1, keepdims=True))
        alpha = jnp.exp(m - m_new)
        p = jnp.exp(s - m_new).astype(BF16)
        acc = alpha * acc + jnp.dot(p, v_ref[pl.ds(start, tk), :], preferred_element_type=F32)
        return m_new, acc

    m, acc = lax.fori_loop(1, nk, body, (m, acc))
    o = acc * (1.0 / acc[:, HEAD_DIM:HEAD_DIM + 1])
    lane = lax.broadcasted_iota(jnp.int32, (tq, LANES), 1)
    for a in range(Q_PER_KV // 2):
        even = o[(2 * a) * tq:(2 * a + 1) * tq]
        odd = o[(2 * a + 1) * tq:(2 * a + 2) * tq]
        pair = jnp.where(lane < HEAD_DIM, even, pltpu.roll(odd, HEAD_DIM, 1))
        o_ref[:, a * LANES:(a + 1) * LANES] = pair.astype(BF16)


def _attention(q, kt, v, tq, tk):
    b, _, lp, _ = q.shape
    gw = Q_PER_KV * HEAD_DIM
    return pl.pallas_call(
        functools.partial(_attn_kernel, tk=tk, n_masked=PAD - N_META),
        grid=(b, N_KV_HEADS, lp // tq),
        in_specs=[
            pl.BlockSpec((None, Q_PER_KV, tq, HEAD_DIM), lambda bi, g, i: (bi, g, i, 0)),
            pl.BlockSpec((None, HEAD_DIM, lp), lambda bi, g, i: (bi, g, 0)),
            pl.BlockSpec((None, None, lp, LANES), lambda bi, g, i: (bi, g, 0, 0)),
        ],
        out_specs=pl.BlockSpec((None, tq, gw), lambda bi, g, i: (bi, i, g)),
        out_shape=jax.ShapeDtypeStruct((b, lp, ATTN_WIDTH), BF16),
        compiler_params=pltpu.CompilerParams(
            dimension_semantics=("arbitrary", "arbitrary", "arbitrary"),
            vmem_limit_bytes=VMEM_LIMIT),
        name="attn",
    )(q, kt, v)


def _post_kernel(h_ref, u_ref, up_ref, un_ref, a_ref, wpool_ref, pscale_ref, wo_ref, g2_ref,
                 w1_ref, w2_ref, o_ref, uext_ref, *, first_valid, ff_chunk):
    t = h_ref.shape[0]
    lp = pl.num_programs(1) * t
    row0 = pl.program_id(1) * t
    r_main = row0 + lax.broadcasted_iota(jnp.int32, (t, 1), 0)
    r_halo = lax.broadcasted_iota(jnp.int32, (HALO, 1), 0)
    valid = r_main >= first_valid

    u_main = jnp.where(valid, u_ref[...], 0.0)
    uext_ref[0:HALO] = jnp.where(row0 - HALO + r_halo >= first_valid, up_ref[...], 0.0)
    uext_ref[HALO:HALO + t] = u_main
    uext_ref[HALO + t:] = jnp.where(row0 + t + r_halo < lp, un_ref[...], 0.0)

    pooled = []
    for gi, w in enumerate(POOL_WINDOWS):
        half = w // 2
        lanes = slice(gi * POOL_GROUP, (gi + 1) * POOL_GROUP)
        base = HALO - half
        wsum = uext_ref[base:base + t, lanes]
        for jj in range(1, w):
            wsum = wsum + uext_ref[base + jj:base + jj + t, lanes]
        lo = jnp.maximum(r_main - half, first_valid)
        hi = jnp.minimum(r_main + (w - half), lp)
        cnt = jnp.maximum(hi - lo, 1).astype(F32)
        dlt = wsum / cnt - u_main[:, lanes]
        y = jnp.dot(dlt.astype(BF16), wpool_ref[gi], preferred_element_type=F32)
        pooled.append((y * pscale_ref[:, lanes]).astype(BF16))

    mix = jnp.concatenate(pooled + [a_ref[...]], axis=1)
    h1 = h_ref[...] + jnp.dot(mix, wo_ref[...], preferred_element_type=F32)

    ms = jnp.mean(h1 * h1, axis=-1, keepdims=True)
    n2 = (h1 * lax.rsqrt(ms + EPS) * g2_ref[...]).astype(BF16)
    acc = h1
    for c in range(D_FF // ff_chunk):
        cols = slice(c * ff_chunk, (c + 1) * ff_chunk)
        mid = jnp.dot(n2, w1_ref[:, cols], preferred_element_type=F32)
        act = jnp.square(jnp.maximum(mid, 0.0)).astype(BF16)
        acc = acc + jnp.dot(act, w2_ref[cols, :], preferred_element_type=F32)
    o_ref[...] = jnp.where(valid, acc, 0.0)


def _post(h, u, attn, w_pool, pool_scale, w_out, g2, w1, w2, t_rows, ff_chunk):
    b, lp, d = h.shape
    nblk = lp // t_rows
    per = t_rows // HALO
    n_halo_blocks = lp // HALO
    row_spec = lambda w: pl.BlockSpec((None, t_rows, w), lambda bi, i: (bi, i, 0))
    prev_spec = pl.BlockSpec((None, HALO, POOL_WIDTH),
                             lambda bi, i: (bi, jnp.maximum(i * per - 1, 0), 0))
    next_spec = pl.BlockSpec((None, HALO, POOL_WIDTH),
                             lambda bi, i: (bi, jnp.minimum((i + 1) * per, n_halo_blocks - 1), 0))
    return pl.pallas_call(
        functools.partial(_post_kernel, first_valid=PAD - N_META, ff_chunk=ff_chunk),
        grid=(b, nblk),
        in_specs=[
            row_spec(d), row_spec(POOL_WIDTH), prev_spec, next_spec, row_spec(ATTN_WIDTH),
            _const_spec(w_pool.shape),
            _const_spec((1, POOL_WIDTH)),
            _const_spec((d, d)),
            _const_spec((1, d)),
            _const_spec((d, D_FF)),
            _const_spec((D_FF, d)),
        ],
        out_specs=row_spec(d),
        out_shape=jax.ShapeDtypeStruct((b, lp, d), F32),
        scratch_shapes=[pltpu.VMEM((t_rows + 2 * HALO, POOL_WIDTH), F32)],
        compiler_params=pltpu.CompilerParams(
            dimension_semantics=("arbitrary", "arbitrary"), vmem_limit_bytes=VMEM_LIMIT),
        name="post",
    )(h, u, u, u, attn, w_pool, pool_scale, w_out, g2, w1, w2)


def _rope_tables(n_tok):
    rows = n_tok // GRID_W
    r = jnp.concatenate([jnp.zeros((PAD,), jnp.int32),
                         jnp.repeat(jnp.arange(rows, dtype=jnp.int32), GRID_W)]).astype(F32)
    c = jnp.concatenate([jnp.zeros((PAD,), jnp.int32),
                         jnp.tile(jnp.arange(GRID_W, dtype=jnp.int32), rows)]).astype(F32)
    inv = ROPE_THETA ** (-jnp.arange(ROT_PAIRS, dtype=F32) / ROT_PAIRS)
    ang_r, ang_c = r[:, None] * inv, c[:, None] * inv
    zero = jnp.zeros_like(ang_r)
    cos = jnp.concatenate([jnp.cos(ang_r)] * 2 + [jnp.cos(ang_c)] * 2, axis=1)
    sa = jnp.concatenate([-jnp.sin(ang_r), zero, -jnp.sin(ang_c), zero], axis=1)
    sb = jnp.concatenate([zero, jnp.sin(ang_r), zero, jnp.sin(ang_c)], axis=1)
    two = lambda tbl: jnp.concatenate([tbl, tbl], axis=1)
    return two(cos), two(sa), two(sb)


def _trunk(x, meta_tokens, norm1_g, w_in, q_norm_g, k_norm_g, w_pool, pool_scale, w_out,
           norm2_g, w_mlp_in, w_mlp_out, bd):
    b, n_tok, d = x.shape
    depth = w_in.shape[0]
    lp = PAD + n_tok
    t_rows, tq, tk = _plan(lp)
    cos, sa, sb = _rope_tables(n_tok)
    meta = jnp.broadcast_to(meta_tokens[None].astype(x.dtype), (b, N_META, d))
    h = jnp.concatenate([jnp.zeros((b, PAD - N_META, d), x.dtype), meta, x], axis=1)
    two = lambda g: jnp.concatenate([g, g])[None]
    for l in range(depth):
        u, q, kt, v = _inproj(h, norm1_g[l][None], w_in[l], two(q_norm_g[l]), two(k_norm_g[l]),
                              bd, cos, sa, sb, t_rows)
        attn = _attention(q, kt, v, tq, tk)
        h = _post(h, u, attn, w_pool[l], pool_scale[l][None], w_out[l], norm2_g[l][None],
                  w_mlp_in[l], w_mlp_out[l], t_rows, 1024)
    return h[:, PAD:]


def kernel(x_prompt, x_sample, meta_tokens, norm1_g, w_in, q_norm_g, k_norm_g, w_pool,
           pool_scale, w_out, norm2_g, w_mlp_in, w_mlp_out):
    head = jnp.arange(LANES, dtype=jnp.int32) // HEAD_DIM
    bd = (head[:, None] == head[None, :]).astype(BF16)
    params = (meta_tokens, norm1_g, w_in.astype(BF16), q_norm_g, k_norm_g, w_pool.astype(BF16),
              pool_scale, w_out.astype(BF16), norm2_g, w_mlp_in.astype(BF16),
              w_mlp_out.astype(BF16), bd)
    return (_trunk(x_prompt, *params), _trunk(x_sample, *params))
```

```python
import functools
import math
from typing import NamedTuple

import jax
import jax.numpy as jnp
from jax import lax
from jax.experimental import pallas as pl
from jax.experimental.pallas import tpu as pltpu

D_MODEL = 1024
N_META = 16
GRID_W = 64
POOL_WIDTH = 512
POOL_WINDOWS = (2, 4, 8, 16)
POOL_GROUP = 128
HEAD_DIM = 64
N_HEADS = 8
N_KV_HEADS = 2
Q_PER_KV = N_HEADS // N_KV_HEADS
ATTN_WIDTH = N_HEADS * HEAD_DIM
KV_WIDTH = N_KV_HEADS * HEAD_DIM
IN_WIDTH = POOL_WIDTH + ATTN_WIDTH + 2 * KV_WIDTH
ROT_PAIRS = HEAD_DIM // 4
ROPE_THETA = 10000.0
D_FF = 4 * D_MODEL
EPS = 1e-6

LANES = 128
SUBLANES = 8
PAD = LANES
HALO = SUBLANES
VMEM_LIMIT = 56 * 1024 * 1024
MAX_UNSHIFTED_SCORE = 40.0
BF16_ROUNDING_SLACK = 1.02

F32 = jnp.float32
BF16 = jnp.bfloat16


def _pick_block(n, cap):
    best = LANES
    for t in range(LANES, cap + 1, LANES):
        if n % t == 0:
            best = t
    return best


class _Plan(NamedTuple):
    t_rows: int
    tq: int
    tk: int
    sub: int
    n_sub: int


def _plan(lp):
    n_tok = lp - PAD
    sub = min(512, n_tok)
    n_sub = min(8, n_tok // sub)
    assert n_tok % (sub * n_sub) == 0, n_tok
    return _Plan(_pick_block(lp, 768), LANES, _pick_block(lp, 768), sub, n_sub)


def _const_spec(shape):
    nd = len(shape)
    return pl.BlockSpec(shape, lambda *_: (0,) * nd, pipeline_mode=pl.Buffered(1))


def _inproj_kernel(h_ref, g1_ref, win_ref, qg_ref, kg_ref, bd_ref, cos_ref, sa_ref, sb_ref,
                   u_ref, q_ref, kt_ref, v_ref):
    x = h_ref[...]
    ms = jnp.mean(x * x, axis=-1, keepdims=True)
    n = (x * lax.rsqrt(ms + EPS) * g1_ref[...]).astype(BF16)
    proj = jnp.dot(n, win_ref[...], preferred_element_type=F32)
    u_ref[...] = proj[:, :POOL_WIDTH]

    cos, sa, sb, bd = cos_ref[...], sa_ref[...], sb_ref[...], bd_ref[...]

    def norm_rope(xc, g):
        sq = xc * xc
        hi = sq.astype(BF16)
        lo = (sq - hi.astype(F32)).astype(BF16)
        ss = (jnp.dot(hi, bd, preferred_element_type=F32)
              + jnp.dot(lo, bd, preferred_element_type=F32))
        y = xc * lax.rsqrt(ss * (1.0 / HEAD_DIM) + EPS) * g
        return (y * cos + pltpu.roll(y, LANES - ROT_PAIRS, 1) * sa
                + pltpu.roll(y, ROT_PAIRS, 1) * sb)

    scale = 1.0 / math.sqrt(HEAD_DIM)
    for c in range(ATTN_WIDTH // LANES):
        lo_col = POOL_WIDTH + c * LANES
        qc = norm_rope(proj[:, lo_col:lo_col + LANES], qg_ref[...]) * scale
        q_ref[2 * c] = qc[:, :HEAD_DIM].astype(BF16)
        q_ref[2 * c + 1] = qc[:, HEAD_DIM:].astype(BF16)

    k0 = POOL_WIDTH + ATTN_WIDTH
    kc = norm_rope(proj[:, k0:k0 + KV_WIDTH], kg_ref[...])
    kt_ref[...] = kc.T.astype(BF16)

    vv = proj[:, k0 + KV_WIDTH:]
    lane = lax.broadcasted_iota(jnp.int32, vv.shape, 1)
    ones_col = jnp.where(lane == HEAD_DIM, 1.0, 0.0)
    v_ref[0] = jnp.where(lane < HEAD_DIM, vv, ones_col).astype(BF16)
    v_ref[1] = jnp.where(lane < HEAD_DIM, pltpu.roll(vv, HEAD_DIM, 1), ones_col).astype(BF16)


def _inproj(h, g1, w_in, qg, kg, bd, cos, sa, sb, t_rows):
    b, lp, d = h.shape
    nblk = lp // t_rows
    row_spec = lambda w: pl.BlockSpec((None, t_rows, w), lambda bi, i: (bi, i, 0))
    tab_spec = pl.BlockSpec((t_rows, LANES), lambda bi, i: (i, 0))
    return pl.pallas_call(
        _inproj_kernel,
        grid=(b, nblk),
        in_specs=[
            row_spec(d),
            _const_spec((1, d)),
            _const_spec((d, IN_WIDTH)),
            _const_spec((1, LANES)),
            _const_spec((1, LANES)),
            _const_spec((LANES, LANES)),
            tab_spec, tab_spec, tab_spec,
        ],
        out_specs=[
            row_spec(POOL_WIDTH),
            pl.BlockSpec((None, N_HEADS, t_rows, HEAD_DIM), lambda bi, i: (bi, 0, i, 0)),
            pl.BlockSpec((None, KV_WIDTH, t_rows), lambda bi, i: (bi, 0, i)),
            pl.BlockSpec((None, N_KV_HEADS, t_rows, LANES), lambda bi, i: (bi, 0, i, 0)),
        ],
        out_shape=[
            jax.ShapeDtypeStruct((b, lp, POOL_WIDTH), F32),
            jax.ShapeDtypeStruct((b, N_HEADS, lp, HEAD_DIM), BF16),
            jax.ShapeDtypeStruct((b, KV_WIDTH, lp), BF16),
            jax.ShapeDtypeStruct((b, N_KV_HEADS, lp, LANES), BF16),
        ],
        compiler_params=pltpu.CompilerParams(
            dimension_semantics=("arbitrary", "arbitrary"), vmem_limit_bytes=VMEM_LIMIT),
        name="inproj",
    )(h, g1, w_in, qg, kg, bd, cos, sa, sb)


def _attn_kernel(q_ref, kt_ref, v_ref, o_ref, *, tk, n_masked):
    tq = q_ref.shape[1]
    nk = kt_ref.shape[1] // tk
    qs = q_ref[...].reshape(Q_PER_KV * tq, HEAD_DIM)

    def scores(start):
        return jnp.dot(qs, kt_ref[:, pl.ds(start, tk)], preferred_element_type=F32)

    s = scores(0)
    col = lax.broadcasted_iota(jnp.int32, s.shape, 1)
    s = jnp.where(col >= n_masked, s, -jnp.inf)
    m = jnp.max(s, axis=1, keepdims=True)
    p = jnp.exp(s - m).astype(BF16)
    acc = jnp.dot(p, v_ref[0:tk, :], preferred_element_type=F32)

    def body(j, carry):
        m, acc = carry
        start = pl.multiple_of(j * tk, tk)
        s = scores(start)
        m_new = jnp.maximum(m, jnp.max(s, axis=1, keepdims=True))
        alpha = jnp.exp(m - m_new)
        p = jnp.exp(s - m_new).astype(BF16)
        acc = alpha * acc + jnp.dot(p, v_ref[pl.ds(start, tk), :], preferred_element_type=F32)
        return m_new, acc

    m, acc = lax.fori_loop(1, nk, body, (m, acc))
    o = acc * (1.0 / acc[:, HEAD_DIM:HEAD_DIM + 1])
    lane = lax.broadcasted_iota(jnp.int32, (tq, LANES), 1)
    for a in range(Q_PER_KV // 2):
        even = o[(2 * a) * tq:(2 * a + 1) * tq]
        odd = o[(2 * a + 1) * tq:(2 * a + 2) * tq]
        pair = jnp.where(lane < HEAD_DIM, even, pltpu.roll(odd, HEAD_DIM, 1))
        o_ref[:, a * LANES:(a + 1) * LANES] = pair.astype(BF16)


def _attn_bounded_kernel(q_ref, kt_ref, v_ref, o_ref, *, sub, n_sub, n_masked):
    tq = q_ref.shape[1]
    chunk = sub * n_sub
    n_chunks = (kt_ref.shape[1] - PAD) // chunk
    qs = q_ref[...].reshape(Q_PER_KV * tq, HEAD_DIM)

    s = jnp.dot(qs, kt_ref[:, 0:PAD], preferred_element_type=F32)
    col = lax.broadcasted_iota(jnp.int32, s.shape, 1)
    p = jnp.where(col >= n_masked, jnp.exp(s), 0.0).astype(BF16)
    acc = jnp.dot(p, v_ref[0:PAD, :], preferred_element_type=F32)

    def body(j, acc):
        base = PAD + j * chunk
        for si in range(n_sub):
            start = pl.multiple_of(base + si * sub, LANES)
            s = jnp.dot(qs, kt_ref[:, pl.ds(start, sub)], preferred_element_type=F32)
            p = jnp.exp(s).astype(BF16)
            acc = acc + jnp.dot(p, v_ref[pl.ds(start, sub), :], preferred_element_type=F32)
        return acc

    acc = lax.fori_loop(0, n_chunks, body, acc)
    o = acc * (1.0 / acc[:, HEAD_DIM:HEAD_DIM + 1])
    lane = lax.broadcasted_iota(jnp.int32, (tq, LANES), 1)
    for a in range(Q_PER_KV // 2):
        even = o[(2 * a) * tq:(2 * a + 1) * tq]
        odd = o[(2 * a + 1) * tq:(2 * a + 2) * tq]
        pair = jnp.where(lane < HEAD_DIM, even, pltpu.roll(odd, HEAD_DIM, 1))
        o_ref[:, a * LANES:(a + 1) * LANES] = pair.astype(BF16)


def _attention(q, kt, v, plan, bounded):
    b, _, lp, _ = q.shape
    gw = Q_PER_KV * HEAD_DIM
    tq = plan.tq
    if bounded:
        body = functools.partial(_attn_bounded_kernel, sub=plan.sub, n_sub=plan.n_sub,
                                 n_masked=PAD - N_META)
    else:
        body = functools.partial(_attn_kernel, tk=plan.tk, n_masked=PAD - N_META)
    return pl.pallas_call(
        body,
        grid=(b, N_KV_HEADS, lp // tq),
        in_specs=[
            pl.BlockSpec((None, Q_PER_KV, tq, HEAD_DIM), lambda bi, g, i: (bi, g, i, 0)),
            pl.BlockSpec((None, HEAD_DIM, lp), lambda bi, g, i: (bi, g, 0)),
            pl.BlockSpec((None, None, lp, LANES), lambda bi, g, i: (bi, g, 0, 0)),
        ],
        out_specs=pl.BlockSpec((None, tq, gw), lambda bi, g, i: (bi, i, g)),
        out_shape=jax.ShapeDtypeStruct((b, lp, ATTN_WIDTH), BF16),
        compiler_params=pltpu.CompilerParams(
            dimension_semantics=("arbitrary", "arbitrary", "arbitrary"),
            vmem_limit_bytes=VMEM_LIMIT),
        name="attn_bounded" if bounded else "attn_online",
    )(q, kt, v)


def _post_kernel(h_ref, u_ref, up_ref, un_ref, a_ref, wpool_ref, pscale_ref, wo_ref, g2_ref,
                 w1_ref, w2_ref, o_ref, uext_ref, *, first_valid, ff_chunk):
    t = h_ref.shape[0]
    lp = pl.num_programs(1) * t
    row0 = pl.program_id(1) * t
    r_main = row0 + lax.broadcasted_iota(jnp.int32, (t, 1), 0)
    r_halo = lax.broadcasted_iota(jnp.int32, (HALO, 1), 0)
    valid = r_main >= first_valid

    u_main = jnp.where(valid, u_ref[...], 0.0)
    uext_ref[0:HALO] = jnp.where(row0 - HALO + r_halo >= first_valid, up_ref[...], 0.0)
    uext_ref[HALO:HALO + t] = u_main
    uext_ref[HALO + t:] = jnp.where(row0 + t + r_halo < lp, un_ref[...], 0.0)

    pooled = []
    for gi, w in enumerate(POOL_WINDOWS):
        half = w // 2
        lanes = slice(gi * POOL_GROUP, (gi + 1) * POOL_GROUP)
        base = HALO - half
        wsum = uext_ref[base:base + t, lanes]
        for jj in range(1, w):
            wsum = wsum + uext_ref[base + jj:base + jj + t, lanes]
        lo = jnp.maximum(r_main - half, first_valid)
        hi = jnp.minimum(r_main + (w - half), lp)
        cnt = jnp.maximum(hi - lo, 1).astype(F32)
        dlt = wsum / cnt - u_main[:, lanes]
        y = jnp.dot(dlt.astype(BF16), wpool_ref[gi], preferred_element_type=F32)
        pooled.append((y * pscale_ref[:, lanes]).astype(BF16))

    mix = jnp.concatenate(pooled + [a_ref[...]], axis=1)
    h1 = h_ref[...] + jnp.dot(mix, wo_ref[...], preferred_element_type=F32)

    ms = jnp.mean(h1 * h1, axis=-1, keepdims=True)
    n2 = (h1 * lax.rsqrt(ms + EPS) * g2_ref[...]).astype(BF16)
    acc = h1
    for c in range(D_FF // ff_chunk):
        cols = slice(c * ff_chunk, (c + 1) * ff_chunk)
        mid = jnp.dot(n2, w1_ref[:, cols], preferred_element_type=F32)
        act = jnp.square(jnp.maximum(mid, 0.0)).astype(BF16)
        acc = acc + jnp.dot(act, w2_ref[cols, :], preferred_element_type=F32)
    o_ref[...] = jnp.where(valid, acc, 0.0)


def _post(h, u, attn, w_pool, pool_scale, w_out, g2, w1, w2, t_rows, ff_chunk):
    b, lp, d = h.shape
    nblk = lp // t_rows
    per = t_rows // HALO
    n_halo_blocks = lp // HALO
    row_spec = lambda w: pl.BlockSpec((None, t_rows, w), lambda bi, i: (bi, i, 0))
    prev_spec = pl.BlockSpec((None, HALO, POOL_WIDTH),
                             lambda bi, i: (bi, jnp.maximum(i * per - 1, 0), 0))
    next_spec = pl.BlockSpec((None, HALO, POOL_WIDTH),
                             lambda bi, i: (bi, jnp.minimum((i + 1) * per, n_halo_blocks - 1), 0))
    return pl.pallas_call(
        functools.partial(_post_kernel, first_valid=PAD - N_META, ff_chunk=ff_chunk),
        grid=(b, nblk),
        in_specs=[
            row_spec(d), row_spec(POOL_WIDTH), prev_spec, next_spec, row_spec(ATTN_WIDTH),
            _const_spec(w_pool.shape),
            _const_spec((1, POOL_WIDTH)),
            _const_spec((d, d)),
            _const_spec((1, d)),
            _const_spec((d, D_FF)),
            _const_spec((D_FF, d)),
        ],
        out_specs=row_spec(d),
        out_shape=jax.ShapeDtypeStruct((b, lp, d), F32),
        scratch_shapes=[pltpu.VMEM((t_rows + 2 * HALO, POOL_WIDTH), F32)],
        compiler_params=pltpu.CompilerParams(
            dimension_semantics=("arbitrary", "arbitrary"), vmem_limit_bytes=VMEM_LIMIT),
        name="post",
    )(h, u, u, u, attn, w_pool, pool_scale, w_out, g2, w1, w2)


def _rope_tables(n_tok):
    rows = n_tok // GRID_W
    r = jnp.concatenate([jnp.zeros((PAD,), jnp.int32),
                         jnp.repeat(jnp.arange(rows, dtype=jnp.int32), GRID_W)]).astype(F32)
    c = jnp.concatenate([jnp.zeros((PAD,), jnp.int32),
                         jnp.tile(jnp.arange(GRID_W, dtype=jnp.int32), rows)]).astype(F32)
    inv = ROPE_THETA ** (-jnp.arange(ROT_PAIRS, dtype=F32) / ROT_PAIRS)
    ang_r, ang_c = r[:, None] * inv, c[:, None] * inv
    zero = jnp.zeros_like(ang_r)
    cos = jnp.concatenate([jnp.cos(ang_r)] * 2 + [jnp.cos(ang_c)] * 2, axis=1)
    sa = jnp.concatenate([-jnp.sin(ang_r), zero, -jnp.sin(ang_c), zero], axis=1)
    sb = jnp.concatenate([zero, jnp.sin(ang_r), zero, jnp.sin(ang_c)], axis=1)
    two = lambda tbl: jnp.concatenate([tbl, tbl], axis=1)
    return two(cos), two(sa), two(sb)


def _trunk(x, meta_tokens, norm1_g, w_in, q_norm_g, k_norm_g, w_pool, pool_scale, w_out,
           norm2_g, w_mlp_in, w_mlp_out, bd):
    b, n_tok, d = x.shape
    depth = w_in.shape[0]
    lp = PAD + n_tok
    plan = _plan(lp)
    t_rows = plan.t_rows
    cos, sa, sb = _rope_tables(n_tok)
    meta = jnp.broadcast_to(meta_tokens[None].astype(x.dtype), (b, N_META, d))
    h = jnp.concatenate([jnp.zeros((b, PAD - N_META, d), x.dtype), meta, x], axis=1)
    two = lambda g: jnp.concatenate([g, g])[None]
    for l in range(depth):
        u, q, kt, v = _inproj(h, norm1_g[l][None], w_in[l], two(q_norm_g[l]), two(k_norm_g[l]),
                              bd, cos, sa, sb, t_rows)
        score_bound = (math.sqrt(HEAD_DIM) * BF16_ROUNDING_SLACK
                       * jnp.max(jnp.abs(q_norm_g[l])) * jnp.max(jnp.abs(k_norm_g[l])))
        attn = lax.cond(score_bound <= MAX_UNSHIFTED_SCORE,
                        functools.partial(_attention, plan=plan, bounded=True),
                        functools.partial(_attention, plan=plan, bounded=False),
                        q, kt, v)
        h = _post(h, u, attn, w_pool[l], pool_scale[l][None], w_out[l], norm2_g[l][None],
                  w_mlp_in[l], w_mlp_out[l], t_rows, 1024)
    return h[:, PAD:]


def kernel(x_prompt, x_sample, meta_tokens, norm1_g, w_in, q_norm_g, k_norm_g, w_pool,
           pool_scale, w_out, norm2_g, w_mlp_in, w_mlp_out):
    head = jnp.arange(LANES, dtype=jnp.int32) // HEAD_DIM
    bd = (head[:, None] == head[None, :]).astype(BF16)
    params = (meta_tokens, norm1_g, w_in.astype(BF16), q_norm_g, k_norm_g, w_pool.astype(BF16),
              pool_scale, w_out.astype(BF16), norm2_g, w_mlp_in.astype(BF16),
              w_mlp_out.astype(BF16), bd)
    return (_trunk(x_prompt, *params), _trunk(x_sample, *params))
```

```python
import functools
import math
from typing import NamedTuple

import jax
import jax.numpy as jnp
from jax import lax
from jax.experimental import pallas as pl
from jax.experimental.pallas import tpu as pltpu

D_MODEL = 1024
N_META = 16
GRID_W = 64
POOL_WIDTH = 512
POOL_WINDOWS = (2, 4, 8, 16)
POOL_GROUP = 128
HEAD_DIM = 64
N_HEADS = 8
N_KV_HEADS = 2
Q_PER_KV = N_HEADS // N_KV_HEADS
ATTN_WIDTH = N_HEADS * HEAD_DIM
KV_WIDTH = N_KV_HEADS * HEAD_DIM
IN_WIDTH = POOL_WIDTH + ATTN_WIDTH + 2 * KV_WIDTH
ROT_PAIRS = HEAD_DIM // 4
ROPE_THETA = 10000.0
D_FF = 4 * D_MODEL
EPS = 1e-6

LANES = 128
SUBLANES = 8
TAIL = LANES
HALO = SUBLANES
VMEM_LIMIT = 56 * 1024 * 1024
MAX_UNSHIFTED_SCORE = 40.0
BF16_ROUNDING_SLACK = 1.02

F32 = jnp.float32
BF16 = jnp.bfloat16


def _pick_block(n, cap):
    best = LANES
    for t in range(LANES, cap + 1, LANES):
        if n % t == 0:
            best = t
    return best


class _Plan(NamedTuple):
    t_rows: int
    tq: int
    tk: int
    sub: int
    n_sub: int


def _plan(n_tok):
    sub = min(512, n_tok)
    n_sub = min(8, n_tok // sub)
    assert n_tok % (sub * n_sub) == 0, n_tok
    return _Plan(_pick_block(n_tok, 512), _pick_block(n_tok, 256), _pick_block(n_tok, 512),
                 sub, n_sub)


def _const_spec(shape):
    nd = len(shape)
    return pl.BlockSpec(shape, lambda *_: (0,) * nd, pipeline_mode=pl.Buffered(1))


def _inproj_kernel(h_ref, g1_ref, win_ref, qg_ref, kg_ref, bd_ref, cos_ref, sa_ref, sb_ref,
                   u_ref, q_ref, kt_ref, v_ref):
    x = h_ref[...]
    ms = jnp.mean(x * x, axis=-1, keepdims=True)
    n = (x * lax.rsqrt(ms + EPS) * g1_ref[...]).astype(BF16)
    proj = jnp.dot(n, win_ref[...], preferred_element_type=F32)
    u_ref[...] = proj[:, :POOL_WIDTH]

    cos, sa, sb, bd = cos_ref[...], sa_ref[...], sb_ref[...], bd_ref[...]

    def norm_rope(xc, g):
        sq = xc * xc
        hi = sq.astype(BF16)
        lo = (sq - hi.astype(F32)).astype(BF16)
        ss = (jnp.dot(hi, bd, preferred_element_type=F32)
              + jnp.dot(lo, bd, preferred_element_type=F32))
        y = xc * lax.rsqrt(ss * (1.0 / HEAD_DIM) + EPS) * g
        return (y * cos + pltpu.roll(y, LANES - ROT_PAIRS, 1) * sa
                + pltpu.roll(y, ROT_PAIRS, 1) * sb)

    scale = 1.0 / math.sqrt(HEAD_DIM)
    for c in range(ATTN_WIDTH // LANES):
        lo_col = POOL_WIDTH + c * LANES
        qc = norm_rope(proj[:, lo_col:lo_col + LANES], qg_ref[...]) * scale
        q_ref[2 * c] = qc[:, :HEAD_DIM].astype(BF16)
        q_ref[2 * c + 1] = qc[:, HEAD_DIM:].astype(BF16)

    k0 = POOL_WIDTH + ATTN_WIDTH
    kc = norm_rope(proj[:, k0:k0 + KV_WIDTH], kg_ref[...])
    kt_ref[...] = kc.T.astype(BF16)

    vv = proj[:, k0 + KV_WIDTH:]
    lane = lax.broadcasted_iota(jnp.int32, vv.shape, 1)
    ones_col = jnp.where(lane == HEAD_DIM, 1.0, 0.0)
    v_ref[0] = jnp.where(lane < HEAD_DIM, vv, ones_col).astype(BF16)
    v_ref[1] = jnp.where(lane < HEAD_DIM, pltpu.roll(vv, HEAD_DIM, 1), ones_col).astype(BF16)


def _inproj(h, g1, w_in, qg, kg, bd, cos, sa, sb, t_rows):
    b, rows, d = h.shape
    row_spec = lambda w: pl.BlockSpec((None, t_rows, w), lambda bi, i: (bi, i, 0))
    tab_spec = pl.BlockSpec((t_rows, LANES), lambda bi, i: (i, 0))
    return pl.pallas_call(
        _inproj_kernel,
        grid=(b, rows // t_rows),
        in_specs=[
            row_spec(d),
            _const_spec((1, d)),
            _const_spec((d, IN_WIDTH)),
            _const_spec((1, LANES)),
            _const_spec((1, LANES)),
            _const_spec((LANES, LANES)),
            tab_spec, tab_spec, tab_spec,
        ],
        out_specs=[
            row_spec(POOL_WIDTH),
            pl.BlockSpec((None, N_HEADS, t_rows, HEAD_DIM), lambda bi, i: (bi, 0, i, 0)),
            pl.BlockSpec((None, KV_WIDTH, t_rows), lambda bi, i: (bi, 0, i)),
            pl.BlockSpec((None, N_KV_HEADS, t_rows, LANES), lambda bi, i: (bi, 0, i, 0)),
        ],
        out_shape=[
            jax.ShapeDtypeStruct((b, rows, POOL_WIDTH), F32),
            jax.ShapeDtypeStruct((b, N_HEADS, rows, HEAD_DIM), BF16),
            jax.ShapeDtypeStruct((b, KV_WIDTH, rows), BF16),
            jax.ShapeDtypeStruct((b, N_KV_HEADS, rows, LANES), BF16),
        ],
        compiler_params=pltpu.CompilerParams(
            dimension_semantics=("arbitrary", "arbitrary"), vmem_limit_bytes=VMEM_LIMIT),
        name="inproj",
    )(h, g1, w_in, qg, kg, bd, cos, sa, sb)


def _write_heads(o_ref, o, tq):
    lane = lax.broadcasted_iota(jnp.int32, (tq, LANES), 1)
    for a in range(Q_PER_KV // 2):
        even = o[(2 * a) * tq:(2 * a + 1) * tq]
        odd = o[(2 * a + 1) * tq:(2 * a + 2) * tq]
        pair = jnp.where(lane < HEAD_DIM, even, pltpu.roll(odd, HEAD_DIM, 1))
        o_ref[:, a * LANES:(a + 1) * LANES] = pair.astype(BF16)


def _attn_online_kernel(q_ref, kt_ref, ktt_ref, v_ref, vt_ref, o_ref, *, tk):
    tq = q_ref.shape[1]
    nk = kt_ref.shape[1] // tk
    qs = q_ref[...].reshape(Q_PER_KV * tq, HEAD_DIM)

    s = jnp.dot(qs, ktt_ref[...], preferred_element_type=F32)
    col = lax.broadcasted_iota(jnp.int32, s.shape, 1)
    s = jnp.where(col < N_META, s, -jnp.inf)
    m = jnp.max(s, axis=1, keepdims=True)
    p = jnp.exp(s - m).astype(BF16)
    acc = jnp.dot(p, vt_ref[...], preferred_element_type=F32)

    def body(j, carry):
        m, acc = carry
        start = pl.multiple_of(j * tk, tk)
        s = jnp.dot(qs, kt_ref[:, pl.ds(start, tk)], preferred_element_type=F32)
        m_new = jnp.maximum(m, jnp.max(s, axis=1, keepdims=True))
        alpha = jnp.exp(m - m_new)
        p = jnp.exp(s - m_new).astype(BF16)
        acc = alpha * acc + jnp.dot(p, v_ref[pl.ds(start, tk), :], preferred_element_type=F32)
        return m_new, acc

    m, acc = lax.fori_loop(0, nk, body, (m, acc))
    _write_heads(o_ref, acc * (1.0 / acc[:, HEAD_DIM:HEAD_DIM + 1]), tq)


def _attn_bounded_kernel(q_ref, kt_ref, ktt_ref, v_ref, vt_ref, o_ref, *, sub, n_sub):
    tq = q_ref.shape[1]
    chunk = sub * n_sub
    n_chunks = kt_ref.shape[1] // chunk
    qs = q_ref[...].reshape(Q_PER_KV * tq, HEAD_DIM)

    s = jnp.dot(qs, ktt_ref[...], preferred_element_type=F32)
    col = lax.broadcasted_iota(jnp.int32, s.shape, 1)
    p = jnp.where(col < N_META, jnp.exp(s), 0.0).astype(BF16)
    acc = jnp.dot(p, vt_ref[...], preferred_element_type=F32)

    def body(j, acc):
        base = j * chunk
        for si in range(n_sub):
            start = pl.multiple_of(base + si * sub, LANES)
            s = jnp.dot(qs, kt_ref[:, pl.ds(start, sub)], preferred_element_type=F32)
            p = jnp.exp(s).astype(BF16)
            acc = acc + jnp.dot(p, v_ref[pl.ds(start, sub), :], preferred_element_type=F32)
        return acc

    acc = lax.fori_loop(0, n_chunks, body, acc)
    _write_heads(o_ref, acc * (1.0 / acc[:, HEAD_DIM:HEAD_DIM + 1]), tq)


def _attention(q, kt, ktt, v, vt, tq, plan, bounded):
    b, _, rows, _ = q.shape
    n_tok = kt.shape[2]
    gw = Q_PER_KV * HEAD_DIM
    if bounded:
        body = functools.partial(_attn_bounded_kernel, sub=plan.sub, n_sub=plan.n_sub)
    else:
        body = functools.partial(_attn_online_kernel, tk=plan.tk)
    return pl.pallas_call(
        body,
        grid=(b, N_KV_HEADS, rows // tq),
        in_specs=[
            pl.BlockSpec((None, Q_PER_KV, tq, HEAD_DIM), lambda bi, g, i: (bi, g, i, 0)),
            pl.BlockSpec((None, HEAD_DIM, n_tok), lambda bi, g, i: (bi, g, 0)),
            pl.BlockSpec((None, HEAD_DIM, TAIL), lambda bi, g, i: (bi, g, 0)),
            pl.BlockSpec((None, None, n_tok, LANES), lambda bi, g, i: (bi, g, 0, 0)),
            pl.BlockSpec((None, None, TAIL, LANES), lambda bi, g, i: (bi, g, 0, 0)),
        ],
        out_specs=pl.BlockSpec((None, tq, gw), lambda bi, g, i: (bi, i, g)),
        out_shape=jax.ShapeDtypeStruct((b, rows, ATTN_WIDTH), BF16),
        compiler_params=pltpu.CompilerParams(
            dimension_semantics=("arbitrary", "arbitrary", "arbitrary"),
            vmem_limit_bytes=VMEM_LIMIT),
        name="attn_bounded" if bounded else "attn_online",
    )(q, kt, ktt, v, vt)


def _post_kernel(h_ref, u_ref, upa_ref, upb_ref, un_ref, a_ref, wpool_ref, pscale_ref, wo_ref,
                 g2_ref, w1_ref, w2_ref, o_ref, uext_ref, *, pos_base, seq_len, n_out, ff_chunk):
    t = h_ref.shape[0]
    step = pl.program_id(1)
    pos0 = pos_base + step * t
    r_main = lax.broadcasted_iota(jnp.int32, (t, 1), 0)
    r_halo = lax.broadcasted_iota(jnp.int32, (HALO, 1), 0)
    pos = pos0 + r_main

    u_main = u_ref[...]
    u_prev = jnp.where(step == 0, upb_ref[...], upa_ref[...])
    uext_ref[0:HALO] = jnp.where(pos0 - HALO + r_halo >= 0, u_prev, 0.0)
    uext_ref[HALO:HALO + t] = u_main
    uext_ref[HALO + t:] = jnp.where(pos0 + t + r_halo < seq_len, un_ref[...], 0.0)

    pooled = []
    for gi, w in enumerate(POOL_WINDOWS):
        half = w // 2
        lanes = slice(gi * POOL_GROUP, (gi + 1) * POOL_GROUP)
        base = HALO - half
        wsum = uext_ref[base:base + t, lanes]
        for jj in range(1, w):
            wsum = wsum + uext_ref[base + jj:base + jj + t, lanes]
        cnt = (jnp.minimum(pos + (w - half), seq_len) - jnp.maximum(pos - half, 0)).astype(F32)
        dlt = wsum / cnt - u_main[:, lanes]
        y = jnp.dot(dlt.astype(BF16), wpool_ref[gi], preferred_element_type=F32)
        pooled.append((y * pscale_ref[:, lanes]).astype(BF16))

    mix = jnp.concatenate(pooled + [a_ref[...]], axis=1)
    h1 = h_ref[...] + jnp.dot(mix, wo_ref[...], preferred_element_type=F32)

    ms = jnp.mean(h1 * h1, axis=-1, keepdims=True)
    n2 = (h1 * lax.rsqrt(ms + EPS) * g2_ref[...]).astype(BF16)
    acc = h1
    for c in range(D_FF // ff_chunk):
        cols = slice(c * ff_chunk, (c + 1) * ff_chunk)
        mid = jnp.dot(n2, w1_ref[:, cols], preferred_element_type=F32)
        act = jnp.square(jnp.maximum(mid, 0.0)).astype(BF16)
        acc = acc + jnp.dot(act, w2_ref[cols, :], preferred_element_type=F32)
    if n_out < t:
        acc = jnp.where(r_main < n_out, acc, 0.0)
    o_ref[...] = acc


def _post(h, u, u_prev_first, u_next, next_block, attn, w_pool, pool_scale, w_out, g2, w1, w2,
          t_rows, pos_base, seq_len, n_out):
    b, rows, d = h.shape
    per = t_rows // HALO
    last_halo = u_next.shape[1] // HALO - 1
    row_spec = lambda w: pl.BlockSpec((None, t_rows, w), lambda bi, i: (bi, i, 0))
    halo_spec = lambda fn: pl.BlockSpec((None, HALO, POOL_WIDTH), fn)
    if next_block is None:
        next_fn = lambda bi, i: (bi, jnp.minimum((i + 1) * per, last_halo), 0)
    else:
        next_fn = lambda bi, i: (bi, next_block, 0)
    return pl.pallas_call(
        functools.partial(_post_kernel, pos_base=pos_base, seq_len=seq_len, n_out=n_out,
                          ff_chunk=1024),
        grid=(b, rows // t_rows),
        in_specs=[
            row_spec(d), row_spec(POOL_WIDTH),
            halo_spec(lambda bi, i: (bi, jnp.maximum(i * per - 1, 0), 0)),
            halo_spec(lambda bi, i: (bi, 1, 0)),
            halo_spec(next_fn),
            row_spec(ATTN_WIDTH),
            _const_spec(w_pool.shape),
            _const_spec((1, POOL_WIDTH)),
            _const_spec((d, d)),
            _const_spec((1, d)),
            _const_spec((d, D_FF)),
            _const_spec((D_FF, d)),
        ],
        out_specs=row_spec(d),
        out_shape=jax.ShapeDtypeStruct((b, rows, d), F32),
        scratch_shapes=[pltpu.VMEM((t_rows + 2 * HALO, POOL_WIDTH), F32)],
        compiler_params=pltpu.CompilerParams(
            dimension_semantics=("arbitrary", "arbitrary"), vmem_limit_bytes=VMEM_LIMIT),
        name="post",
    )(h, u, u, u_prev_first, u_next, attn, w_pool, pool_scale, w_out, g2, w1, w2)


def _rope_tables(n_tok):
    rows = n_tok // GRID_W
    inv = ROPE_THETA ** (-jnp.arange(ROT_PAIRS, dtype=F32) / ROT_PAIRS)
    ang_r = jnp.arange(rows, dtype=jnp.int32).astype(F32)[:, None] * inv
    ang_c = jnp.arange(GRID_W, dtype=jnp.int32).astype(F32)[:, None] * inv
    by_row = lambda tbl: jnp.repeat(tbl, GRID_W, axis=0)
    by_col = lambda tbl: jnp.tile(tbl, (rows, 1))
    cos_r, sin_r = by_row(jnp.cos(ang_r)), by_row(jnp.sin(ang_r))
    cos_c, sin_c = by_col(jnp.cos(ang_c)), by_col(jnp.sin(ang_c))
    zero = jnp.zeros_like(cos_r)
    cos = jnp.concatenate([cos_r, cos_r, cos_c, cos_c], axis=1)
    sa = jnp.concatenate([-sin_r, zero, -sin_c, zero], axis=1)
    sb = jnp.concatenate([zero, sin_r, zero, sin_c], axis=1)
    two = lambda tbl: jnp.concatenate([tbl, tbl], axis=1)
    return two(cos), two(sa), two(sb)


def _trunk(x, tables, meta_tokens, norm1_g, w_in, q_norm_g, k_norm_g, w_pool, pool_scale, w_out,
           norm2_g, w_mlp_in, w_mlp_out, bd):
    b, n_tok, d = x.shape
    depth = w_in.shape[0]
    seq_len = N_META + n_tok
    plan = _plan(n_tok)
    cos, sa, sb = tables
    tail_cos, tail_sin = jnp.ones((TAIL, LANES), F32), jnp.zeros((TAIL, LANES), F32)
    meta = jnp.broadcast_to(meta_tokens[None].astype(x.dtype), (b, N_META, d))
    h_main = x
    h_tail = jnp.concatenate([meta, jnp.zeros((b, TAIL - N_META, d), x.dtype)], axis=1)
    two = lambda g: jnp.concatenate([g, g])[None]
    for l in range(depth):
        last = l == depth - 1
        shared = (norm1_g[l][None], w_in[l], two(q_norm_g[l]), two(k_norm_g[l]), bd)
        u, q, kt, v = _inproj(h_main, *shared, cos, sa, sb, plan.t_rows)
        ut, qt, ktt, vt = _inproj(h_tail, *shared, tail_cos, tail_sin, tail_sin, TAIL)

        score_bound = (math.sqrt(HEAD_DIM) * BF16_ROUNDING_SLACK
                       * jnp.max(jnp.abs(q_norm_g[l])) * jnp.max(jnp.abs(k_norm_g[l])))

        def attend(bounded, q, qt, kt, ktt, v, vt, last=last):
            a_main = _attention(q, kt, ktt, v, vt, plan.tq, plan, bounded)
            if last:
                return (a_main,)
            return a_main, _attention(qt, kt, ktt, v, vt, TAIL, plan, bounded)

        attn = lax.cond(score_bound <= MAX_UNSHIFTED_SCORE,
                        functools.partial(attend, True), functools.partial(attend, False),
                        q, qt, kt, ktt, v, vt)

        weights = (w_pool[l], pool_scale[l][None], w_out[l], norm2_g[l][None],
                   w_mlp_in[l], w_mlp_out[l])
        if not last:
            u_prefix = jnp.concatenate([ut[:, :N_META], u[:, :TAIL - N_META]], axis=1)
            h_tail = _post(h_tail, u_prefix, u_prefix, u, (TAIL - N_META) // HALO, attn[1],
                           *weights, TAIL, 0, seq_len, N_META)
        h_main = _post(h_main, u, ut, u, None, attn[0], *weights, plan.t_rows, N_META, seq_len,
                       plan.t_rows)
    return h_main


def kernel(x_prompt, x_sample, meta_tokens, norm1_g, w_in, q_norm_g, k_norm_g, w_pool,
           pool_scale, w_out, norm2_g, w_mlp_in, w_mlp_out):
    head = jnp.arange(LANES, dtype=jnp.int32) // HEAD_DIM
    bd = (head[:, None] == head[None, :]).astype(BF16)
    params = (meta_tokens, norm1_g, w_in.astype(BF16), q_norm_g, k_norm_g, w_pool.astype(BF16),
              pool_scale, w_out.astype(BF16), norm2_g, w_mlp_in.astype(BF16),
              w_mlp_out.astype(BF16), bd)
    tables = _rope_tables(max(x_prompt.shape[1], x_sample.shape[1]))
    return (_trunk(x_prompt, tables, *params), _trunk(x_sample, tables, *params))
```

```python
import functools
import math
from typing import NamedTuple

import jax
import jax.numpy as jnp
from jax import lax
from jax.experimental import pallas as pl
from jax.experimental.pallas import tpu as pltpu

D_MODEL = 1024
N_META = 16
GRID_W = 64
POOL_WIDTH = 512
POOL_WINDOWS = (2, 4, 8, 16)
POOL_GROUP = 128
HEAD_DIM = 64
N_HEADS = 8
N_KV_HEADS = 2
Q_PER_KV = N_HEADS // N_KV_HEADS
ATTN_WIDTH = N_HEADS * HEAD_DIM
KV_WIDTH = N_KV_HEADS * HEAD_DIM
IN_WIDTH = POOL_WIDTH + ATTN_WIDTH + 2 * KV_WIDTH
ROT_PAIRS = HEAD_DIM // 4
ROPE_THETA = 10000.0
D_FF = 4 * D_MODEL
EPS = 1e-6

LANES = 128
SUBLANES = 8
TAIL = LANES
HALO = SUBLANES
VMEM_LIMIT = 56 * 1024 * 1024
MAX_UNSHIFTED_SCORE = 40.0
BF16_ROUNDING_SLACK = 1.02

F32 = jnp.float32
BF16 = jnp.bfloat16


def _pick_block(n, cap):
    best = LANES
    for t in range(LANES, cap + 1, LANES):
        if n % t == 0:
            best = t
    return best


class _Plan(NamedTuple):
    t_rows: int
    tq: int
    tk: int
    sub: int
    n_sub: int


def _plan(n_tok):
    sub = min(512, n_tok)
    n_sub = min(8, n_tok // sub)
    assert n_tok % (sub * n_sub) == 0, n_tok
    return _Plan(_pick_block(n_tok, 512), _pick_block(n_tok, 512), _pick_block(n_tok, 512),
                 sub, n_sub)


def _const_spec(shape):
    nd = len(shape)
    return pl.BlockSpec(shape, lambda *_: (0,) * nd, pipeline_mode=pl.Buffered(1))


def _inproj_rows(rows, h_ref, g1_ref, win_ref, qg_ref, kg_ref, bd_ref, cos_ref, sa_ref, sb_ref,
                 u_ref, q_ref, kt_ref, v_ref):
    x = h_ref[rows, :]
    ms = jnp.mean(x * x, axis=-1, keepdims=True)
    n = (x * lax.rsqrt(ms + EPS) * g1_ref[...]).astype(BF16)
    proj = jnp.dot(n, win_ref[...], preferred_element_type=F32)
    u_ref[rows, :] = proj[:, :POOL_WIDTH]

    cos, sa, sb, bd = cos_ref[rows, :], sa_ref[rows, :], sb_ref[rows, :], bd_ref[...]

    def norm_rope(xc, g):
        sq = xc * xc
        hi = sq.astype(BF16)
        lo = (sq - hi.astype(F32)).astype(BF16)
        ss = (jnp.dot(hi, bd, preferred_element_type=F32)
              + jnp.dot(lo, bd, preferred_element_type=F32))
        y = xc * lax.rsqrt(ss * (1.0 / HEAD_DIM) + EPS) * g
        return (y * cos + pltpu.roll(y, LANES - ROT_PAIRS, 1) * sa
                + pltpu.roll(y, ROT_PAIRS, 1) * sb)

    scale = 1.0 / math.sqrt(HEAD_DIM)
    for c in range(ATTN_WIDTH // LANES):
        lo_col = POOL_WIDTH + c * LANES
        qc = norm_rope(proj[:, lo_col:lo_col + LANES], qg_ref[...]) * scale
        q_ref[2 * c, rows, :] = qc[:, :HEAD_DIM].astype(BF16)
        q_ref[2 * c + 1, rows, :] = qc[:, HEAD_DIM:].astype(BF16)

    k0 = POOL_WIDTH + ATTN_WIDTH
    kc = norm_rope(proj[:, k0:k0 + KV_WIDTH], kg_ref[...])
    kt_ref[:, rows] = kc.T.astype(BF16)

    vv = proj[:, k0 + KV_WIDTH:]
    lane = lax.broadcasted_iota(jnp.int32, vv.shape, 1)
    ones_col = jnp.where(lane == HEAD_DIM, 1.0, 0.0)
    v_ref[0, rows, :] = jnp.where(lane < HEAD_DIM, vv, ones_col).astype(BF16)
    v_ref[1, rows, :] = jnp.where(lane < HEAD_DIM, pltpu.roll(vv, HEAD_DIM, 1),
                                  ones_col).astype(BF16)


def _inproj_kernel(h_ref, *refs):
    t = h_ref.shape[0]
    n_part = 2 if t % (2 * LANES) == 0 else 1
    for pi in range(n_part):
        _inproj_rows(slice(pi * (t // n_part), (pi + 1) * (t // n_part)), h_ref, *refs)


def _inproj(h, g1, w_in, qg, kg, bd, cos, sa, sb, t_rows):
    b, rows, d = h.shape
    row_spec = lambda w: pl.BlockSpec((None, t_rows, w), lambda bi, i: (bi, i, 0))
    tab_spec = pl.BlockSpec((t_rows, LANES), lambda bi, i: (i, 0))
    return pl.pallas_call(
        _inproj_kernel,
        grid=(b, rows // t_rows),
        in_specs=[
            row_spec(d),
            _const_spec((1, d)),
            _const_spec((d, IN_WIDTH)),
            _const_spec((1, LANES)),
            _const_spec((1, LANES)),
            _const_spec((LANES, LANES)),
            tab_spec, tab_spec, tab_spec,
        ],
        out_specs=[
            row_spec(POOL_WIDTH),
            pl.BlockSpec((None, N_HEADS, t_rows, HEAD_DIM), lambda bi, i: (bi, 0, i, 0)),
            pl.BlockSpec((None, KV_WIDTH, t_rows), lambda bi, i: (bi, 0, i)),
            pl.BlockSpec((None, N_KV_HEADS, t_rows, LANES), lambda bi, i: (bi, 0, i, 0)),
        ],
        out_shape=[
            jax.ShapeDtypeStruct((b, rows, POOL_WIDTH), F32),
            jax.ShapeDtypeStruct((b, N_HEADS, rows, HEAD_DIM), BF16),
            jax.ShapeDtypeStruct((b, KV_WIDTH, rows), BF16),
            jax.ShapeDtypeStruct((b, N_KV_HEADS, rows, LANES), BF16),
        ],
        compiler_params=pltpu.CompilerParams(
            dimension_semantics=("arbitrary", "arbitrary"), vmem_limit_bytes=VMEM_LIMIT),
        name="inproj",
    )(h, g1, w_in, qg, kg, bd, cos, sa, sb)


def _write_heads(o_ref, o, tq):
    lane = lax.broadcasted_iota(jnp.int32, (tq, LANES), 1)
    for a in range(Q_PER_KV // 2):
        even = o[(2 * a) * tq:(2 * a + 1) * tq]
        odd = o[(2 * a + 1) * tq:(2 * a + 2) * tq]
        pair = jnp.where(lane < HEAD_DIM, even, pltpu.roll(odd, HEAD_DIM, 1))
        o_ref[:, a * LANES:(a + 1) * LANES] = pair.astype(BF16)


def _attn_online_kernel(q_ref, kt_ref, ktt_ref, v_ref, vt_ref, o_ref, *, tk):
    tq = q_ref.shape[1]
    nk = kt_ref.shape[1] // tk
    qs = q_ref[...].reshape(Q_PER_KV * tq, HEAD_DIM)

    s = jnp.dot(qs, ktt_ref[...], preferred_element_type=F32)
    col = lax.broadcasted_iota(jnp.int32, s.shape, 1)
    s = jnp.where(col < N_META, s, -jnp.inf)
    m = jnp.max(s, axis=1, keepdims=True)
    p = jnp.exp(s - m).astype(BF16)
    acc = jnp.dot(p, vt_ref[...], preferred_element_type=F32)

    def body(j, carry):
        m, acc = carry
        start = pl.multiple_of(j * tk, tk)
        s = jnp.dot(qs, kt_ref[:, pl.ds(start, tk)], preferred_element_type=F32)
        m_new = jnp.maximum(m, jnp.max(s, axis=1, keepdims=True))
        alpha = jnp.exp(m - m_new)
        p = jnp.exp(s - m_new).astype(BF16)
        acc = alpha * acc + jnp.dot(p, v_ref[pl.ds(start, tk), :], preferred_element_type=F32)
        return m_new, acc

    m, acc = lax.fori_loop(0, nk, body, (m, acc))
    _write_heads(o_ref, acc * (1.0 / acc[:, HEAD_DIM:HEAD_DIM + 1]), tq)


def _attn_bounded_kernel(q_ref, kt_ref, ktt_ref, v_ref, vt_ref, o_ref, *, sub, n_sub):
    tq = q_ref.shape[1]
    chunk = sub * n_sub
    n_chunks = kt_ref.shape[1] // chunk
    qs = q_ref[...].reshape(Q_PER_KV * tq, HEAD_DIM)

    s = jnp.dot(qs, ktt_ref[...], preferred_element_type=F32)
    col = lax.broadcasted_iota(jnp.int32, s.shape, 1)
    p = jnp.where(col < N_META, jnp.exp(s), 0.0).astype(BF16)
    acc = jnp.dot(p, vt_ref[...], preferred_element_type=F32)

    def body(j, acc):
        base = j * chunk
        for si in range(n_sub):
            start = pl.multiple_of(base + si * sub, LANES)
            s = jnp.dot(qs, kt_ref[:, pl.ds(start, sub)], preferred_element_type=F32)
            p = jnp.exp(s).astype(BF16)
            acc = acc + jnp.dot(p, v_ref[pl.ds(start, sub), :], preferred_element_type=F32)
        return acc

    acc = lax.fori_loop(0, n_chunks, body, acc)
    _write_heads(o_ref, acc * (1.0 / acc[:, HEAD_DIM:HEAD_DIM + 1]), tq)


def _attention(q, kt, ktt, v, vt, rows, tq, plan, bounded):
    b = q.shape[0]
    n_tok = kt.shape[2]
    gw = Q_PER_KV * HEAD_DIM
    if bounded:
        body = functools.partial(_attn_bounded_kernel, sub=plan.sub, n_sub=plan.n_sub)
    else:
        body = functools.partial(_attn_online_kernel, tk=plan.tk)
    return pl.pallas_call(
        body,
        grid=(b, N_KV_HEADS, rows // tq),
        in_specs=[
            pl.BlockSpec((None, Q_PER_KV, tq, HEAD_DIM), lambda bi, g, i: (bi, g, i, 0)),
            pl.BlockSpec((None, HEAD_DIM, n_tok), lambda bi, g, i: (bi, g, 0)),
            pl.BlockSpec((None, HEAD_DIM, TAIL), lambda bi, g, i: (bi, g, 0)),
            pl.BlockSpec((None, None, n_tok, LANES), lambda bi, g, i: (bi, g, 0, 0)),
            pl.BlockSpec((None, None, TAIL, LANES), lambda bi, g, i: (bi, g, 0, 0)),
        ],
        out_specs=pl.BlockSpec((None, tq, gw), lambda bi, g, i: (bi, i, g)),
        out_shape=jax.ShapeDtypeStruct((b, rows, ATTN_WIDTH), BF16),
        compiler_params=pltpu.CompilerParams(
            dimension_semantics=("arbitrary", "arbitrary", "arbitrary"),
            vmem_limit_bytes=VMEM_LIMIT),
        name="attn_bounded" if bounded else "attn_online",
    )(q, kt, ktt, v, vt)


def _post_kernel(h_ref, u_ref, upa_ref, upb_ref, un_ref, a_ref, wpool_ref, pscale_ref, wo_ref,
                 g2_ref, w1_ref, w2_ref, o_ref, uext_ref, *, pos_base, seq_len, n_out, ff_chunk):
    t = h_ref.shape[0]
    step = pl.program_id(1)
    pos0 = pos_base + step * t
    r_main = lax.broadcasted_iota(jnp.int32, (t, 1), 0)
    r_halo = lax.broadcasted_iota(jnp.int32, (HALO, 1), 0)
    pos = pos0 + r_main

    u_main = u_ref[...]
    u_prev = jnp.where(step == 0, upb_ref[...], upa_ref[...])
    uext_ref[0:HALO] = jnp.where(pos0 - HALO + r_halo >= 0, u_prev, 0.0)
    uext_ref[HALO:HALO + t] = u_main
    uext_ref[HALO + t:] = jnp.where(pos0 + t + r_halo < seq_len, un_ref[...], 0.0)

    pooled = []
    for gi, w in enumerate(POOL_WINDOWS):
        half = w // 2
        lanes = slice(gi * POOL_GROUP, (gi + 1) * POOL_GROUP)
        base = HALO - half
        wsum = uext_ref[base:base + t, lanes]
        for jj in range(1, w):
            wsum = wsum + uext_ref[base + jj:base + jj + t, lanes]
        cnt = (jnp.minimum(pos + (w - half), seq_len) - jnp.maximum(pos - half, 0)).astype(F32)
        dlt = wsum / cnt - u_main[:, lanes]
        y = jnp.dot(dlt.astype(BF16), wpool_ref[gi], preferred_element_type=F32)
        pooled.append((y * pscale_ref[:, lanes]).astype(BF16))

    attn = a_ref[...]
    if attn.shape[0] < t:
        attn = jnp.concatenate([attn, jnp.zeros((t - attn.shape[0], ATTN_WIDTH), BF16)], axis=0)
    mix = jnp.concatenate(pooled + [attn], axis=1)
    h1 = h_ref[...] + jnp.dot(mix, wo_ref[...], preferred_element_type=F32)

    ms = jnp.mean(h1 * h1, axis=-1, keepdims=True)
    n2 = (h1 * lax.rsqrt(ms + EPS) * g2_ref[...]).astype(BF16)
    acc = h1
    for c in range(D_FF // ff_chunk):
        cols = slice(c * ff_chunk, (c + 1) * ff_chunk)
        mid = jnp.dot(n2, w1_ref[:, cols], preferred_element_type=F32)
        act = jnp.square(jnp.maximum(mid, 0.0)).astype(BF16)
        acc = acc + jnp.dot(act, w2_ref[cols, :], preferred_element_type=F32)
    if n_out < t:
        acc = jnp.where(r_main < n_out, acc, 0.0)
    o_ref[...] = acc


def _post(h, u, u_prev_first, u_next, next_block, attn, w_pool, pool_scale, w_out, g2, w1, w2,
          t_rows, pos_base, seq_len, n_out):
    b, rows, d = h.shape
    per = t_rows // HALO
    last_halo = u_next.shape[1] // HALO - 1
    row_spec = lambda w: pl.BlockSpec((None, t_rows, w), lambda bi, i: (bi, i, 0))
    halo_spec = lambda fn: pl.BlockSpec((None, HALO, POOL_WIDTH), fn)
    if next_block is None:
        next_fn = lambda bi, i: (bi, jnp.minimum((i + 1) * per, last_halo), 0)
    else:
        next_fn = lambda bi, i: (bi, next_block, 0)
    return pl.pallas_call(
        functools.partial(_post_kernel, pos_base=pos_base, seq_len=seq_len, n_out=n_out,
                          ff_chunk=1024),
        grid=(b, rows // t_rows),
        in_specs=[
            row_spec(d), row_spec(POOL_WIDTH),
            halo_spec(lambda bi, i: (bi, jnp.maximum(i * per - 1, 0), 0)),
            halo_spec(lambda bi, i: (bi, 1, 0)),
            halo_spec(next_fn),
            pl.BlockSpec((None, min(t_rows, attn.shape[1]), ATTN_WIDTH),
                         lambda bi, i: (bi, i, 0)),
            _const_spec(w_pool.shape),
            _const_spec((1, POOL_WIDTH)),
            _const_spec((d, d)),
            _const_spec((1, d)),
            _const_spec((d, D_FF)),
            _const_spec((D_FF, d)),
        ],
        out_specs=row_spec(d),
        out_shape=jax.ShapeDtypeStruct((b, rows, d), F32),
        scratch_shapes=[pltpu.VMEM((t_rows + 2 * HALO, POOL_WIDTH), F32)],
        compiler_params=pltpu.CompilerParams(
            dimension_semantics=("arbitrary", "arbitrary"), vmem_limit_bytes=VMEM_LIMIT),
        name="post",
    )(h, u, u, u_prev_first, u_next, attn, w_pool, pool_scale, w_out, g2, w1, w2)


def _rope_tables(n_tok):
    rows = n_tok // GRID_W
    inv = ROPE_THETA ** (-jnp.arange(ROT_PAIRS, dtype=F32) / ROT_PAIRS)
    ang_r = jnp.arange(rows, dtype=jnp.int32).astype(F32)[:, None] * inv
    ang_c = jnp.arange(GRID_W, dtype=jnp.int32).astype(F32)[:, None] * inv
    by_row = lambda tbl: jnp.repeat(tbl, GRID_W, axis=0)
    by_col = lambda tbl: jnp.tile(tbl, (rows, 1))
    cos_r, sin_r = by_row(jnp.cos(ang_r)), by_row(jnp.sin(ang_r))
    cos_c, sin_c = by_col(jnp.cos(ang_c)), by_col(jnp.sin(ang_c))
    zero = jnp.zeros_like(cos_r)
    cos = jnp.concatenate([cos_r, cos_r, cos_c, cos_c], axis=1)
    sa = jnp.concatenate([-sin_r, zero, -sin_c, zero], axis=1)
    sb = jnp.concatenate([zero, sin_r, zero, sin_c], axis=1)
    two = lambda tbl: jnp.concatenate([tbl, tbl], axis=1)
    return two(cos), two(sa), two(sb)


def _trunk(x, tables, meta_tokens, norm1_g, w_in, q_norm_g, k_norm_g, w_pool, pool_scale, w_out,
           norm2_g, w_mlp_in, w_mlp_out, bd):
    b, n_tok, d = x.shape
    depth = w_in.shape[0]
    seq_len = N_META + n_tok
    plan = _plan(n_tok)
    cos, sa, sb = tables
    tail_cos, tail_sin = jnp.ones((TAIL, LANES), F32), jnp.zeros((TAIL, LANES), F32)
    meta = jnp.broadcast_to(meta_tokens[None].astype(x.dtype), (b, N_META, d))
    h_main = x
    h_tail = jnp.concatenate([meta, jnp.zeros((b, TAIL - N_META, d), x.dtype)], axis=1)
    two = lambda g: jnp.concatenate([g, g])[None]
    for l in range(depth):
        last = l == depth - 1
        shared = (norm1_g[l][None], w_in[l], two(q_norm_g[l]), two(k_norm_g[l]), bd)
        u, q, kt, v = _inproj(h_main, *shared, cos, sa, sb, plan.t_rows)
        ut, qt, ktt, vt = _inproj(h_tail, *shared, tail_cos, tail_sin, tail_sin, TAIL)

        score_bound = (math.sqrt(HEAD_DIM) * BF16_ROUNDING_SLACK
                       * jnp.max(jnp.abs(q_norm_g[l])) * jnp.max(jnp.abs(k_norm_g[l])))

        def attend(bounded, q, qt, kt, ktt, v, vt, last=last):
            a_main = _attention(q, kt, ktt, v, vt, n_tok, plan.tq, plan, bounded)
            if last:
                return (a_main,)
            return a_main, _attention(qt, kt, ktt, v, vt, N_META, N_META, plan, bounded)

        attn = lax.cond(score_bound <= MAX_UNSHIFTED_SCORE,
                        functools.partial(attend, True), functools.partial(attend, False),
                        q, qt, kt, ktt, v, vt)

        weights = (w_pool[l], pool_scale[l][None], w_out[l], norm2_g[l][None],
                   w_mlp_in[l], w_mlp_out[l])
        if not last:
            u_prefix = jnp.concatenate([ut[:, :N_META], u[:, :TAIL - N_META]], axis=1)
            h_tail = _post(h_tail, u_prefix, u_prefix, u, (TAIL - N_META) // HALO, attn[1],
                           *weights, TAIL, 0, seq_len, N_META)
        h_main = _post(h_main, u, ut, u, None, attn[0], *weights, plan.t_rows, N_META, seq_len,
                       plan.t_rows)
    return h_main


def kernel(x_prompt, x_sample, meta_tokens, norm1_g, w_in, q_norm_g, k_norm_g, w_pool,
           pool_scale, w_out, norm2_g, w_mlp_in, w_mlp_out):
    head = jnp.arange(LANES, dtype=jnp.int32) // HEAD_DIM
    bd = (head[:, None] == head[None, :]).astype(BF16)
    params = (meta_tokens, norm1_g, w_in.astype(BF16), q_norm_g, k_norm_g, w_pool.astype(BF16),
              pool_scale, w_out.astype(BF16), norm2_g, w_mlp_in.astype(BF16),
              w_mlp_out.astype(BF16), bd)
    tables = _rope_tables(max(x_prompt.shape[1], x_sample.shape[1]))
    return (_trunk(x_prompt, tables, *params), _trunk(x_sample, tables, *params))
```

```python
import functools
import math
from typing import NamedTuple

import jax
import jax.numpy as jnp
from jax import lax
from jax.experimental import pallas as pl
from jax.experimental.pallas import tpu as pltpu

D_MODEL = 1024
N_META = 16
GRID_W = 64
POOL_WIDTH = 512
POOL_WINDOWS = (2, 4, 8, 16)
POOL_GROUP = 128
HEAD_DIM = 64
N_HEADS = 8
N_KV_HEADS = 2
Q_PER_KV = N_HEADS // N_KV_HEADS
ATTN_WIDTH = N_HEADS * HEAD_DIM
KV_WIDTH = N_KV_HEADS * HEAD_DIM
IN_WIDTH = POOL_WIDTH + ATTN_WIDTH + 2 * KV_WIDTH
ROT_PAIRS = HEAD_DIM // 4
ROPE_THETA = 10000.0
D_FF = 4 * D_MODEL
EPS = 1e-6

LANES = 128
SUBLANES = 8
TAIL = LANES
HALO = SUBLANES
BF16_TILE_ROWS = 16
V_ROWS = HEAD_DIM + BF16_TILE_ROWS
VMEM_LIMIT = 56 * 1024 * 1024
MAX_UNSHIFTED_SCORE = 40.0
BF16_ROUNDING_SLACK = 1.02

F32 = jnp.float32
BF16 = jnp.bfloat16


def _pick_block(n, cap):
    best = LANES
    for t in range(LANES, cap + 1, LANES):
        if n % t == 0:
            best = t
    return best


class _Plan(NamedTuple):
    t_rows: int
    tq: int
    tk: int
    sub: int
    n_sub: int


def _plan(n_tok):
    sub = min(512, n_tok)
    n_sub = min(8, n_tok // sub)
    assert n_tok % (sub * n_sub) == 0, n_tok
    return _Plan(_pick_block(n_tok, 512), _pick_block(n_tok, 512), _pick_block(n_tok, 512),
                 sub, n_sub)


def _const_spec(shape):
    nd = len(shape)
    return pl.BlockSpec(shape, lambda *_: (0,) * nd, pipeline_mode=pl.Buffered(1))


def _inproj_rows(rows, h_ref, g1_ref, win_ref, qg_ref, kg_ref, bd_ref, cos_ref, sa_ref, sb_ref,
                 u_ref, qt_ref, k_ref, vt_ref):
    x = h_ref[rows, :]
    ms = jnp.mean(x * x, axis=-1, keepdims=True)
    n = (x * lax.rsqrt(ms + EPS) * g1_ref[...]).astype(BF16)
    proj = jnp.dot(n, win_ref[...], preferred_element_type=F32)
    u_ref[rows, :] = proj[:, :POOL_WIDTH]

    cos, sa, sb, bd = cos_ref[rows, :], sa_ref[rows, :], sb_ref[rows, :], bd_ref[...]

    def norm_rope(xc, g):
        sq = xc * xc
        hi = sq.astype(BF16)
        lo = (sq - hi.astype(F32)).astype(BF16)
        ss = (jnp.dot(hi, bd, preferred_element_type=F32)
              + jnp.dot(lo, bd, preferred_element_type=F32))
        y = xc * lax.rsqrt(ss * (1.0 / HEAD_DIM) + EPS) * g
        return (y * cos + pltpu.roll(y, LANES - ROT_PAIRS, 1) * sa
                + pltpu.roll(y, ROT_PAIRS, 1) * sb)

    scale = 1.0 / math.sqrt(HEAD_DIM)
    for c in range(ATTN_WIDTH // LANES):
        lo_col = POOL_WIDTH + c * LANES
        qc = norm_rope(proj[:, lo_col:lo_col + LANES], qg_ref[...]) * scale
        qt_ref[c, :, rows] = qc.T.astype(BF16)

    k0 = POOL_WIDTH + ATTN_WIDTH
    kc = norm_rope(proj[:, k0:k0 + KV_WIDTH], kg_ref[...])
    k_ref[rows, :] = kc.astype(BF16)

    vt = proj[:, k0 + KV_WIDTH:].T
    extra = V_ROWS - HEAD_DIM
    ones_row = jnp.where(lax.broadcasted_iota(jnp.int32, (extra, vt.shape[1]), 0) == 0, 1.0, 0.0)
    for g in range(N_KV_HEADS):
        vt_ref[g, :, rows] = jnp.concatenate(
            [vt[g * HEAD_DIM:(g + 1) * HEAD_DIM], ones_row], axis=0).astype(BF16)


def _inproj_kernel(h_ref, *refs):
    t = h_ref.shape[0]
    n_part = 2 if t % (2 * LANES) == 0 else 1
    for pi in range(n_part):
        _inproj_rows(slice(pi * (t // n_part), (pi + 1) * (t // n_part)), h_ref, *refs)


def _inproj(h, g1, w_in, qg, kg, bd, cos, sa, sb, t_rows):
    b, rows, d = h.shape
    row_spec = lambda w: pl.BlockSpec((None, t_rows, w), lambda bi, i: (bi, i, 0))
    tab_spec = pl.BlockSpec((t_rows, LANES), lambda bi, i: (i, 0))
    return pl.pallas_call(
        _inproj_kernel,
        grid=(b, rows // t_rows),
        in_specs=[
            row_spec(d),
            _const_spec((1, d)),
            _const_spec((d, IN_WIDTH)),
            _const_spec((1, LANES)),
            _const_spec((1, LANES)),
            _const_spec((LANES, LANES)),
            tab_spec, tab_spec, tab_spec,
        ],
        out_specs=[
            row_spec(POOL_WIDTH),
            pl.BlockSpec((None, ATTN_WIDTH // LANES, LANES, t_rows), lambda bi, i: (bi, 0, 0, i)),
            row_spec(KV_WIDTH),
            pl.BlockSpec((None, N_KV_HEADS, V_ROWS, t_rows), lambda bi, i: (bi, 0, 0, i)),
        ],
        out_shape=[
            jax.ShapeDtypeStruct((b, rows, POOL_WIDTH), F32),
            jax.ShapeDtypeStruct((b, ATTN_WIDTH // LANES, LANES, rows), BF16),
            jax.ShapeDtypeStruct((b, rows, KV_WIDTH), BF16),
            jax.ShapeDtypeStruct((b, N_KV_HEADS, V_ROWS, rows), BF16),
        ],
        compiler_params=pltpu.CompilerParams(
            dimension_semantics=("arbitrary", "arbitrary"), vmem_limit_bytes=VMEM_LIMIT),
        name="inproj",
    )(h, g1, w_in, qg, kg, bd, cos, sa, sb)


def _stack_queries(qs_ref, qt_ref, tq):
    g = pl.program_id(1)
    qs_ref[...] = jnp.zeros_like(qs_ref)
    row0 = pl.multiple_of(g * HEAD_DIM, HEAD_DIM)
    for i in range(Q_PER_KV):
        half = slice((i % 2) * HEAD_DIM, (i % 2 + 1) * HEAD_DIM)
        qs_ref[pl.ds(row0, HEAD_DIM), i * tq:(i + 1) * tq] = qt_ref[i // 2, half, :]
    return qs_ref[...]


def _write_heads(o_ref, acc, tq):
    o = acc[:HEAD_DIM] * (1.0 / acc[HEAD_DIM:HEAD_DIM + 1])
    for a in range(Q_PER_KV // 2):
        pair = jnp.concatenate([o[:, (2 * a) * tq:(2 * a + 1) * tq],
                                o[:, (2 * a + 1) * tq:(2 * a + 2) * tq]], axis=0)
        o_ref[:, a * LANES:(a + 1) * LANES] = pair.T.astype(BF16)


def _attn_online_kernel(qt_ref, k_ref, ktl_ref, vt_ref, vtl_ref, o_ref, qs_ref, *, tk):
    tq = qt_ref.shape[2]
    nk = k_ref.shape[0] // tk
    qs = _stack_queries(qs_ref, qt_ref, tq)

    st = jnp.dot(ktl_ref[...], qs, preferred_element_type=F32)
    row = lax.broadcasted_iota(jnp.int32, st.shape, 0)
    st = jnp.where(row < N_META, st, -jnp.inf)
    m = jnp.max(st, axis=0, keepdims=True)
    p = jnp.exp(st - m).astype(BF16)
    acc = jnp.dot(vtl_ref[...], p, preferred_element_type=F32)

    def body(j, carry):
        m, acc = carry
        start = pl.multiple_of(j * tk, tk)
        st = jnp.dot(k_ref[pl.ds(start, tk), :], qs, preferred_element_type=F32)
        m_new = jnp.maximum(m, jnp.max(st, axis=0, keepdims=True))
        alpha = jnp.exp(m - m_new)
        p = jnp.exp(st - m_new).astype(BF16)
        acc = alpha * acc + jnp.dot(vt_ref[:, pl.ds(start, tk)], p, preferred_element_type=F32)
        return m_new, acc

    m, acc = lax.fori_loop(0, nk, body, (m, acc))
    _write_heads(o_ref, acc, tq)


def _attn_bounded_kernel(qt_ref, k_ref, ktl_ref, vt_ref, vtl_ref, o_ref, qs_ref, *, sub, n_sub):
    tq = qt_ref.shape[2]
    chunk = sub * n_sub
    n_chunks = k_ref.shape[0] // chunk
    qs = _stack_queries(qs_ref, qt_ref, tq)

    st = jnp.dot(ktl_ref[...], qs, preferred_element_type=F32)
    row = lax.broadcasted_iota(jnp.int32, st.shape, 0)
    p = jnp.where(row < N_META, jnp.exp(st), 0.0).astype(BF16)
    acc = jnp.dot(vtl_ref[...], p, preferred_element_type=F32)

    def body(j, acc):
        base = j * chunk
        for si in range(n_sub):
            start = pl.multiple_of(base + si * sub, LANES)
            st = jnp.dot(k_ref[pl.ds(start, sub), :], qs, preferred_element_type=F32)
            p = jnp.exp(st).astype(BF16)
            acc = acc + jnp.dot(vt_ref[:, pl.ds(start, sub)], p, preferred_element_type=F32)
        return acc

    acc = lax.fori_loop(0, n_chunks, body, acc)
    _write_heads(o_ref, acc, tq)


def _attention(qt, k, ktl, vt, vtl, tq, plan, bounded):
    b, _, _, rows = qt.shape
    n_tok = k.shape[1]
    gw = Q_PER_KV * HEAD_DIM
    if bounded:
        body = functools.partial(_attn_bounded_kernel, sub=plan.sub, n_sub=plan.n_sub)
    else:
        body = functools.partial(_attn_online_kernel, tk=plan.tk)
    return pl.pallas_call(
        body,
        grid=(b, N_KV_HEADS, rows // tq),
        in_specs=[
            pl.BlockSpec((None, Q_PER_KV // 2, LANES, tq), lambda bi, g, i: (bi, g, 0, i)),
            pl.BlockSpec((None, n_tok, KV_WIDTH), lambda bi, g, i: (bi, 0, 0)),
            pl.BlockSpec((None, TAIL, KV_WIDTH), lambda bi, g, i: (bi, 0, 0)),
            pl.BlockSpec((None, None, V_ROWS, n_tok), lambda bi, g, i: (bi, g, 0, 0)),
            pl.BlockSpec((None, None, V_ROWS, TAIL), lambda bi, g, i: (bi, g, 0, 0)),
        ],
        out_specs=pl.BlockSpec((None, tq, gw), lambda bi, g, i: (bi, i, g)),
        out_shape=jax.ShapeDtypeStruct((b, rows, ATTN_WIDTH), BF16),
        scratch_shapes=[pltpu.VMEM((KV_WIDTH, Q_PER_KV * tq), BF16)],
        compiler_params=pltpu.CompilerParams(
            dimension_semantics=("arbitrary", "arbitrary", "arbitrary"),
            vmem_limit_bytes=VMEM_LIMIT),
        name="attn_bounded" if bounded else "attn_online",
    )(qt, k, ktl, vt, vtl)


def _post_kernel(h_ref, u_ref, upa_ref, upb_ref, un_ref, a_ref, wpool_ref, pscale_ref, wo_ref,
                 g2_ref, w1_ref, w2_ref, o_ref, uext_ref, *, pos_base, seq_len, n_out, ff_chunk):
    t = h_ref.shape[0]
    step = pl.program_id(1)
    pos0 = pos_base + step * t
    r_main = lax.broadcasted_iota(jnp.int32, (t, 1), 0)
    r_halo = lax.broadcasted_iota(jnp.int32, (HALO, 1), 0)
    pos = pos0 + r_main

    u_main = u_ref[...]
    u_prev = jnp.where(step == 0, upb_ref[...], upa_ref[...])
    uext_ref[0:HALO] = jnp.where(pos0 - HALO + r_halo >= 0, u_prev, 0.0)
    uext_ref[HALO:HALO + t] = u_main
    uext_ref[HALO + t:] = jnp.where(pos0 + t + r_halo < seq_len, un_ref[...], 0.0)

    pooled = []
    for gi, w in enumerate(POOL_WINDOWS):
        half = w // 2
        lanes = slice(gi * POOL_GROUP, (gi + 1) * POOL_GROUP)
        base = HALO - half
        wsum = uext_ref[base:base + t, lanes]
        for jj in range(1, w):
            wsum = wsum + uext_ref[base + jj:base + jj + t, lanes]
        cnt = (jnp.minimum(pos + (w - half), seq_len) - jnp.maximum(pos - half, 0)).astype(F32)
        dlt = wsum / cnt - u_main[:, lanes]
        y = jnp.dot(dlt.astype(BF16), wpool_ref[gi], preferred_element_type=F32)
        pooled.append((y * pscale_ref[:, lanes]).astype(BF16))

    mix = jnp.concatenate(pooled + [a_ref[...]], axis=1)
    h1 = h_ref[...] + jnp.dot(mix, wo_ref[...], preferred_element_type=F32)

    ms = jnp.mean(h1 * h1, axis=-1, keepdims=True)
    n2 = (h1 * lax.rsqrt(ms + EPS) * g2_ref[...]).astype(BF16)
    acc = h1
    for c in range(D_FF // ff_chunk):
        cols = slice(c * ff_chunk, (c + 1) * ff_chunk)
        mid = jnp.dot(n2, w1_ref[:, cols], preferred_element_type=F32)
        act = jnp.square(jnp.maximum(mid, 0.0)).astype(BF16)
        acc = acc + jnp.dot(act, w2_ref[cols, :], preferred_element_type=F32)
    if n_out < t:
        acc = jnp.where(r_main < n_out, acc, 0.0)
    o_ref[...] = acc


def _post(h, u, u_prev_first, u_next, next_block, attn, w_pool, pool_scale, w_out, g2, w1, w2,
          t_rows, pos_base, seq_len, n_out):
    b, rows, d = h.shape
    per = t_rows // HALO
    last_halo = u_next.shape[1] // HALO - 1
    row_spec = lambda w: pl.BlockSpec((None, t_rows, w), lambda bi, i: (bi, i, 0))
    halo_spec = lambda fn: pl.BlockSpec((None, HALO, POOL_WIDTH), fn)
    if next_block is None:
        next_fn = lambda bi, i: (bi, jnp.minimum((i + 1) * per, last_halo), 0)
    else:
        next_fn = lambda bi, i: (bi, next_block, 0)
    return pl.pallas_call(
        functools.partial(_post_kernel, pos_base=pos_base, seq_len=seq_len, n_out=n_out,
                          ff_chunk=1024),
        grid=(b, rows // t_rows),
        in_specs=[
            row_spec(d), row_spec(POOL_WIDTH),
            halo_spec(lambda bi, i: (bi, jnp.maximum(i * per - 1, 0), 0)),
            halo_spec(lambda bi, i: (bi, 1, 0)),
            halo_spec(next_fn),
            row_spec(ATTN_WIDTH),
            _const_spec(w_pool.shape),
            _const_spec((1, POOL_WIDTH)),
            _const_spec((d, d)),
            _const_spec((1, d)),
            _const_spec((d, D_FF)),
            _const_spec((D_FF, d)),
        ],
        out_specs=row_spec(d),
        out_shape=jax.ShapeDtypeStruct((b, rows, d), F32),
        scratch_shapes=[pltpu.VMEM((t_rows + 2 * HALO, POOL_WIDTH), F32)],
        compiler_params=pltpu.CompilerParams(
            dimension_semantics=("arbitrary", "arbitrary"), vmem_limit_bytes=VMEM_LIMIT),
        name="post",
    )(h, u, u, u_prev_first, u_next, attn, w_pool, pool_scale, w_out, g2, w1, w2)


def _rope_tables(n_tok):
    rows = n_tok // GRID_W
    inv = ROPE_THETA ** (-jnp.arange(ROT_PAIRS, dtype=F32) / ROT_PAIRS)
    ang_r = jnp.arange(rows, dtype=jnp.int32).astype(F32)[:, None] * inv
    ang_c = jnp.arange(GRID_W, dtype=jnp.int32).astype(F32)[:, None] * inv
    by_row = lambda tbl: jnp.repeat(tbl, GRID_W, axis=0)
    by_col = lambda tbl: jnp.tile(tbl, (rows, 1))
    cos_r, sin_r = by_row(jnp.cos(ang_r)), by_row(jnp.sin(ang_r))
    cos_c, sin_c = by_col(jnp.cos(ang_c)), by_col(jnp.sin(ang_c))
    zero = jnp.zeros_like(cos_r)
    cos = jnp.concatenate([cos_r, cos_r, cos_c, cos_c], axis=1)
    sa = jnp.concatenate([-sin_r, zero, -sin_c, zero], axis=1)
    sb = jnp.concatenate([zero, sin_r, zero, sin_c], axis=1)
    two = lambda tbl: jnp.concatenate([tbl, tbl], axis=1)
    return two(cos), two(sa), two(sb)


def _trunk(x, tables, meta_tokens, norm1_g, w_in, q_norm_g, k_norm_g, w_pool, pool_scale, w_out,
           norm2_g, w_mlp_in, w_mlp_out, bd):
    b, n_tok, d = x.shape
    depth = w_in.shape[0]
    seq_len = N_META + n_tok
    plan = _plan(n_tok)
    cos, sa, sb = tables
    tail_cos, tail_sin = jnp.ones((TAIL, LANES), F32), jnp.zeros((TAIL, LANES), F32)
    meta = jnp.broadcast_to(meta_tokens[None].astype(x.dtype), (b, N_META, d))
    h_main = x
    h_tail = jnp.concatenate([meta, jnp.zeros((b, TAIL - N_META, d), x.dtype)], axis=1)
    two = lambda g: jnp.concatenate([g, g])[None]
    for l in range(depth):
        last = l == depth - 1
        shared = (norm1_g[l][None], w_in[l], two(q_norm_g[l]), two(k_norm_g[l]), bd)
        u, qt, k, vt = _inproj(h_main, *shared, cos, sa, sb, plan.t_rows)
        ut, qtl, ktl, vtl = _inproj(h_tail, *shared, tail_cos, tail_sin, tail_sin, TAIL)

        score_bound = (math.sqrt(HEAD_DIM) * BF16_ROUNDING_SLACK
                       * jnp.max(jnp.abs(q_norm_g[l])) * jnp.max(jnp.abs(k_norm_g[l])))

        def attend(bounded, qt, qtl, k, ktl, vt, vtl, last=last):
            a_main = _attention(qt, k, ktl, vt, vtl, plan.tq, plan, bounded)
            if last:
                return (a_main,)
            return a_main, _attention(qtl, k, ktl, vt, vtl, TAIL, plan, bounded)

        attn = lax.cond(score_bound <= MAX_UNSHIFTED_SCORE,
                        functools.partial(attend, True), functools.partial(attend, False),
                        qt, qtl, k, ktl, vt, vtl)

        weights = (w_pool[l], pool_scale[l][None], w_out[l], norm2_g[l][None],
                   w_mlp_in[l], w_mlp_out[l])
        if not last:
            u_prefix = jnp.concatenate([ut[:, :N_META], u[:, :TAIL - N_META]], axis=1)
            h_tail = _post(h_tail, u_prefix, u_prefix, u, (TAIL - N_META) // HALO, attn[1],
                           *weights, TAIL, 0, seq_len, N_META)
        h_main = _post(h_main, u, ut, u, None, attn[0], *weights, plan.t_rows, N_META, seq_len,
                       plan.t_rows)
    return h_main


def kernel(x_prompt, x_sample, meta_tokens, norm1_g, w_in, q_norm_g, k_norm_g, w_pool,
           pool_scale, w_out, norm2_g, w_mlp_in, w_mlp_out):
    head = jnp.arange(LANES, dtype=jnp.int32) // HEAD_DIM
    bd = (head[:, None] == head[None, :]).astype(BF16)
    params = (meta_tokens, norm1_g, w_in.astype(BF16), q_norm_g, k_norm_g, w_pool.astype(BF16),
              pool_scale, w_out.astype(BF16), norm2_g, w_mlp_in.astype(BF16),
              w_mlp_out.astype(BF16), bd)
    tables = _rope_tables(max(x_prompt.shape[1], x_sample.shape[1]))
    return (_trunk(x_prompt, tables, *params), _trunk(x_sample, tables, *params))
```

```python
import functools
import math
from typing import NamedTuple

import jax
import jax.numpy as jnp
from jax import lax
from jax.experimental import pallas as pl
from jax.experimental.pallas import tpu as pltpu

D_MODEL = 1024
N_META = 16
GRID_W = 64
POOL_WIDTH = 512
POOL_WINDOWS = (2, 4, 8, 16)
POOL_GROUP = 128
HEAD_DIM = 64
N_HEADS = 8
N_KV_HEADS = 2
Q_PER_KV = N_HEADS // N_KV_HEADS
ATTN_WIDTH = N_HEADS * HEAD_DIM
KV_WIDTH = N_KV_HEADS * HEAD_DIM
IN_WIDTH = POOL_WIDTH + ATTN_WIDTH + 2 * KV_WIDTH
ROT_PAIRS = HEAD_DIM // 4
ROPE_THETA = 10000.0
D_FF = 4 * D_MODEL
EPS = 1e-6

LANES = 128
SUBLANES = 8
TAIL = LANES
HALO = SUBLANES
BF16_TILE_ROWS = 16
V_ROWS = HEAD_DIM + BF16_TILE_ROWS
VMEM_LIMIT = 56 * 1024 * 1024
MAX_UNSHIFTED_SCORE = 40.0
BF16_ROUNDING_SLACK = 1.02

F32 = jnp.float32
BF16 = jnp.bfloat16


def _pick_block(n, cap):
    best = LANES
    for t in range(LANES, cap + 1, LANES):
        if n % t == 0:
            best = t
    return best


class _Plan(NamedTuple):
    t_rows: int
    tq: int
    tk: int
    sub: int
    n_sub: int


def _plan(n_tok):
    sub = min(256, n_tok)
    n_sub = min(16, n_tok // sub)
    assert n_tok % (sub * n_sub) == 0, n_tok
    return _Plan(_pick_block(n_tok, 512), _pick_block(n_tok, 512), _pick_block(n_tok, 512),
                 sub, n_sub)


def _const_spec(shape):
    nd = len(shape)
    return pl.BlockSpec(shape, lambda *_: (0,) * nd, pipeline_mode=pl.Buffered(1))


def _inproj_rows(rows, h_ref, g1_ref, win_ref, qg_ref, kg_ref, bd_ref, cos_ref, sa_ref, sb_ref,
                 u_ref, qt_ref, k_ref, vt_ref):
    x = h_ref[rows, :]
    ms = jnp.mean(x * x, axis=-1, keepdims=True)
    n = (x * lax.rsqrt(ms + EPS) * g1_ref[...]).astype(BF16)
    proj = jnp.dot(n, win_ref[...], preferred_element_type=F32)
    u_ref[rows, :] = proj[:, :POOL_WIDTH]

    cos, sa, sb, bd = cos_ref[rows, :], sa_ref[rows, :], sb_ref[rows, :], bd_ref[...]

    def norm_rope(xc, g):
        sq = xc * xc
        hi = sq.astype(BF16)
        lo = (sq - hi.astype(F32)).astype(BF16)
        ss = (jnp.dot(hi, bd, preferred_element_type=F32)
              + jnp.dot(lo, bd, preferred_element_type=F32))
        y = xc * lax.rsqrt(ss * (1.0 / HEAD_DIM) + EPS) * g
        return (y * cos + pltpu.roll(y, LANES - ROT_PAIRS, 1) * sa
                + pltpu.roll(y, ROT_PAIRS, 1) * sb)

    scale = 1.0 / math.sqrt(HEAD_DIM)
    for c in range(ATTN_WIDTH // LANES):
        lo_col = POOL_WIDTH + c * LANES
        qc = norm_rope(proj[:, lo_col:lo_col + LANES], qg_ref[...]) * scale
        qt_ref[c, :, rows] = qc.T.astype(BF16)

    k0 = POOL_WIDTH + ATTN_WIDTH
    kc = norm_rope(proj[:, k0:k0 + KV_WIDTH], kg_ref[...])
    k_ref[rows, :] = kc.astype(BF16)

    vt = proj[:, k0 + KV_WIDTH:].T
    extra = V_ROWS - HEAD_DIM
    ones_row = jnp.where(lax.broadcasted_iota(jnp.int32, (extra, vt.shape[1]), 0) == 0, 1.0, 0.0)
    for g in range(N_KV_HEADS):
        vt_ref[g, :, rows] = jnp.concatenate(
            [vt[g * HEAD_DIM:(g + 1) * HEAD_DIM], ones_row], axis=0).astype(BF16)


def _inproj_kernel(h_ref, *refs):
    t = h_ref.shape[0]
    n_part = 2 if t % (2 * LANES) == 0 else 1
    for pi in range(n_part):
        _inproj_rows(slice(pi * (t // n_part), (pi + 1) * (t // n_part)), h_ref, *refs)


def _inproj(h, g1, w_in, qg, kg, bd, cos, sa, sb, t_rows):
    b, rows, d = h.shape
    row_spec = lambda w: pl.BlockSpec((None, t_rows, w), lambda bi, i: (bi, i, 0))
    tab_spec = pl.BlockSpec((t_rows, LANES), lambda bi, i: (i, 0))
    return pl.pallas_call(
        _inproj_kernel,
        grid=(b, rows // t_rows),
        in_specs=[
            row_spec(d),
            _const_spec((1, d)),
            _const_spec((d, IN_WIDTH)),
            _const_spec((1, LANES)),
            _const_spec((1, LANES)),
            _const_spec((LANES, LANES)),
            tab_spec, tab_spec, tab_spec,
        ],
        out_specs=[
            row_spec(POOL_WIDTH),
            pl.BlockSpec((None, ATTN_WIDTH // LANES, LANES, t_rows), lambda bi, i: (bi, 0, 0, i)),
            row_spec(KV_WIDTH),
            pl.BlockSpec((None, N_KV_HEADS, V_ROWS, t_rows), lambda bi, i: (bi, 0, 0, i)),
        ],
        out_shape=[
            jax.ShapeDtypeStruct((b, rows, POOL_WIDTH), F32),
            jax.ShapeDtypeStruct((b, ATTN_WIDTH // LANES, LANES, rows), BF16),
            jax.ShapeDtypeStruct((b, rows, KV_WIDTH), BF16),
            jax.ShapeDtypeStruct((b, N_KV_HEADS, V_ROWS, rows), BF16),
        ],
        compiler_params=pltpu.CompilerParams(
            dimension_semantics=("arbitrary", "arbitrary"), vmem_limit_bytes=VMEM_LIMIT),
        name="inproj",
    )(h, g1, w_in, qg, kg, bd, cos, sa, sb)


def _stack_queries(qs_ref, qt_ref, tq):
    g = pl.program_id(1)
    qs_ref[...] = jnp.zeros_like(qs_ref)
    row0 = pl.multiple_of(g * HEAD_DIM, HEAD_DIM)
    for i in range(Q_PER_KV):
        half = slice((i % 2) * HEAD_DIM, (i % 2 + 1) * HEAD_DIM)
        qs_ref[pl.ds(row0, HEAD_DIM), i * tq:(i + 1) * tq] = qt_ref[i // 2, half, :]
    return qs_ref[...]


def _write_heads(o_ref, acc, tq):
    o = acc[:HEAD_DIM] * (1.0 / acc[HEAD_DIM:HEAD_DIM + 1])
    for a in range(Q_PER_KV // 2):
        pair = jnp.concatenate([o[:, (2 * a) * tq:(2 * a + 1) * tq],
                                o[:, (2 * a + 1) * tq:(2 * a + 2) * tq]], axis=0)
        o_ref[:, a * LANES:(a + 1) * LANES] = pair.T.astype(BF16)


def _attn_online_kernel(qt_ref, k_ref, ktl_ref, vt_ref, vtl_ref, o_ref, qs_ref, *, tk):
    tq = qt_ref.shape[2]
    nk = k_ref.shape[0] // tk
    qs = _stack_queries(qs_ref, qt_ref, tq)

    st = jnp.dot(ktl_ref[...], qs, preferred_element_type=F32)
    row = lax.broadcasted_iota(jnp.int32, st.shape, 0)
    st = jnp.where(row < N_META, st, -jnp.inf)
    m = jnp.max(st, axis=0, keepdims=True)
    p = jnp.exp(st - m).astype(BF16)
    acc = jnp.dot(vtl_ref[...], p, preferred_element_type=F32)

    def body(j, carry):
        m, acc = carry
        start = pl.multiple_of(j * tk, tk)
        st = jnp.dot(k_ref[pl.ds(start, tk), :], qs, preferred_element_type=F32)
        m_new = jnp.maximum(m, jnp.max(st, axis=0, keepdims=True))
        alpha = jnp.exp(m - m_new)
        p = jnp.exp(st - m_new).astype(BF16)
        acc = alpha * acc + jnp.dot(vt_ref[:, pl.ds(start, tk)], p, preferred_element_type=F32)
        return m_new, acc

    m, acc = lax.fori_loop(0, nk, body, (m, acc))
    _write_heads(o_ref, acc, tq)


def _attn_bounded_kernel(qt_ref, k_ref, ktl_ref, vt_ref, vtl_ref, o_ref, qs_ref, *, sub, n_sub):
    tq = qt_ref.shape[2]
    chunk = sub * n_sub
    n_chunks = k_ref.shape[0] // chunk
    qs = _stack_queries(qs_ref, qt_ref, tq)

    st = jnp.dot(ktl_ref[...], qs, preferred_element_type=F32)
    row = lax.broadcasted_iota(jnp.int32, st.shape, 0)
    p = jnp.where(row < N_META, jnp.exp(st), 0.0).astype(BF16)
    acc = jnp.dot(vtl_ref[...], p, preferred_element_type=F32)

    def body(j, acc):
        base = j * chunk
        for si in range(n_sub):
            start = pl.multiple_of(base + si * sub, LANES)
            st = jnp.dot(k_ref[pl.ds(start, sub), :], qs, preferred_element_type=F32)
            p = jnp.exp(st).astype(BF16)
            acc = acc + jnp.dot(vt_ref[:, pl.ds(start, sub)], p, preferred_element_type=F32)
        return acc

    acc = lax.fori_loop(0, n_chunks, body, acc)
    _write_heads(o_ref, acc, tq)


def _attention(qt, k, ktl, vt, vtl, tq, plan, bounded):
    b, _, _, rows = qt.shape
    n_tok = k.shape[1]
    gw = Q_PER_KV * HEAD_DIM
    if bounded:
        body = functools.partial(_attn_bounded_kernel, sub=plan.sub, n_sub=plan.n_sub)
    else:
        body = functools.partial(_attn_online_kernel, tk=plan.tk)
    return pl.pallas_call(
        body,
        grid=(b, N_KV_HEADS, rows // tq),
        in_specs=[
            pl.BlockSpec((None, Q_PER_KV // 2, LANES, tq), lambda bi, g, i: (bi, g, 0, i)),
            pl.BlockSpec((None, n_tok, KV_WIDTH), lambda bi, g, i: (bi, 0, 0)),
            pl.BlockSpec((None, TAIL, KV_WIDTH), lambda bi, g, i: (bi, 0, 0)),
            pl.BlockSpec((None, None, V_ROWS, n_tok), lambda bi, g, i: (bi, g, 0, 0)),
            pl.BlockSpec((None, None, V_ROWS, TAIL), lambda bi, g, i: (bi, g, 0, 0)),
        ],
        out_specs=pl.BlockSpec((None, tq, gw), lambda bi, g, i: (bi, i, g)),
        out_shape=jax.ShapeDtypeStruct((b, rows, ATTN_WIDTH), BF16),
        scratch_shapes=[pltpu.VMEM((KV_WIDTH, Q_PER_KV * tq), BF16)],
        compiler_params=pltpu.CompilerParams(
            dimension_semantics=("arbitrary", "arbitrary", "arbitrary"),
            vmem_limit_bytes=VMEM_LIMIT),
        name="attn_bounded" if bounded else "attn_online",
    )(qt, k, ktl, vt, vtl)


def _post_kernel(h_ref, u_ref, upa_ref, upb_ref, un_ref, a_ref, wpool_ref, pscale_ref, wo_ref,
                 g2_ref, w1_ref, w2_ref, o_ref, uext_ref, *, pos_base, seq_len, n_out, ff_chunk):
    t = h_ref.shape[0]
    step = pl.program_id(1)
    pos0 = pos_base + step * t
    r_main = lax.broadcasted_iota(jnp.int32, (t, 1), 0)
    r_halo = lax.broadcasted_iota(jnp.int32, (HALO, 1), 0)
    pos = pos0 + r_main

    u_main = u_ref[...]
    u_prev = jnp.where(step == 0, upb_ref[...], upa_ref[...])
    uext_ref[0:HALO] = jnp.where(pos0 - HALO + r_halo >= 0, u_prev, 0.0)
    uext_ref[HALO:HALO + t] = u_main
    uext_ref[HALO + t:] = jnp.where(pos0 + t + r_halo < seq_len, un_ref[...], 0.0)

    pooled = []
    for gi, w in enumerate(POOL_WINDOWS):
        half = w // 2
        lanes = slice(gi * POOL_GROUP, (gi + 1) * POOL_GROUP)
        base = HALO - half
        wsum = uext_ref[base:base + t, lanes]
        for jj in range(1, w):
            wsum = wsum + uext_ref[base + jj:base + jj + t, lanes]
        cnt = (jnp.minimum(pos + (w - half), seq_len) - jnp.maximum(pos - half, 0)).astype(F32)
        dlt = wsum / cnt - u_main[:, lanes]
        y = jnp.dot(dlt.astype(BF16), wpool_ref[gi], preferred_element_type=F32)
        pooled.append((y * pscale_ref[:, lanes]).astype(BF16))

    mix = jnp.concatenate(pooled + [a_ref[...]], axis=1)
    h1 = h_ref[...] + jnp.dot(mix, wo_ref[...], preferred_element_type=F32)

    ms = jnp.mean(h1 * h1, axis=-1, keepdims=True)
    n2 = (h1 * lax.rsqrt(ms + EPS) * g2_ref[...]).astype(BF16)
    acc = h1
    for c in range(D_FF // ff_chunk):
        cols = slice(c * ff_chunk, (c + 1) * ff_chunk)
        mid = jnp.dot(n2, w1_ref[:, cols], preferred_element_type=F32)
        act = jnp.square(jnp.maximum(mid, 0.0)).astype(BF16)
        acc = acc + jnp.dot(act, w2_ref[cols, :], preferred_element_type=F32)
    if n_out < t:
        acc = jnp.where(r_main < n_out, acc, 0.0)
    o_ref[...] = acc


def _post(h, u, u_prev_first, u_next, next_block, attn, w_pool, pool_scale, w_out, g2, w1, w2,
          t_rows, pos_base, seq_len, n_out):
    b, rows, d = h.shape
    per = t_rows // HALO
    last_halo = u_next.shape[1] // HALO - 1
    row_spec = lambda w: pl.BlockSpec((None, t_rows, w), lambda bi, i: (bi, i, 0))
    halo_spec = lambda fn: pl.BlockSpec((None, HALO, POOL_WIDTH), fn)
    if next_block is None:
        next_fn = lambda bi, i: (bi, jnp.minimum((i + 1) * per, last_halo), 0)
    else:
        next_fn = lambda bi, i: (bi, next_block, 0)
    return pl.pallas_call(
        functools.partial(_post_kernel, pos_base=pos_base, seq_len=seq_len, n_out=n_out,
                          ff_chunk=1024),
        grid=(b, rows // t_rows),
        in_specs=[
            row_spec(d), row_spec(POOL_WIDTH),
            halo_spec(lambda bi, i: (bi, jnp.maximum(i * per - 1, 0), 0)),
            halo_spec(lambda bi, i: (bi, 1, 0)),
            halo_spec(next_fn),
            row_spec(ATTN_WIDTH),
            _const_spec(w_pool.shape),
            _const_spec((1, POOL_WIDTH)),
            _const_spec((d, d)),
            _const_spec((1, d)),
            _const_spec((d, D_FF)),
            _const_spec((D_FF, d)),
        ],
        out_specs=row_spec(d),
        out_shape=jax.ShapeDtypeStruct((b, rows, d), F32),
        scratch_shapes=[pltpu.VMEM((t_rows + 2 * HALO, POOL_WIDTH), F32)],
        compiler_params=pltpu.CompilerParams(
            dimension_semantics=("arbitrary", "arbitrary"), vmem_limit_bytes=VMEM_LIMIT),
        name="post",
    )(h, u, u, u_prev_first, u_next, attn, w_pool, pool_scale, w_out, g2, w1, w2)


def _rope_tables(n_tok):
    rows = n_tok // GRID_W
    inv = ROPE_THETA ** (-jnp.arange(ROT_PAIRS, dtype=F32) / ROT_PAIRS)
    ang_r = jnp.arange(rows, dtype=jnp.int32).astype(F32)[:, None] * inv
    ang_c = jnp.arange(GRID_W, dtype=jnp.int32).astype(F32)[:, None] * inv
    by_row = lambda tbl: jnp.repeat(tbl, GRID_W, axis=0)
    by_col = lambda tbl: jnp.tile(tbl, (rows, 1))
    cos_r, sin_r = by_row(jnp.cos(ang_r)), by_row(jnp.sin(ang_r))
    cos_c, sin_c = by_col(jnp.cos(ang_c)), by_col(jnp.sin(ang_c))
    zero = jnp.zeros_like(cos_r)
    cos = jnp.concatenate([cos_r, cos_r, cos_c, cos_c], axis=1)
    sa = jnp.concatenate([-sin_r, zero, -sin_c, zero], axis=1)
    sb = jnp.concatenate([zero, sin_r, zero, sin_c], axis=1)
    two = lambda tbl: jnp.concatenate([tbl, tbl], axis=1)
    return two(cos), two(sa), two(sb)


def _trunk(x, tables, meta_tokens, norm1_g, w_in, q_norm_g, k_norm_g, w_pool, pool_scale, w_out,
           norm2_g, w_mlp_in, w_mlp_out, bd):
    b, n_tok, d = x.shape
    depth = w_in.shape[0]
    seq_len = N_META + n_tok
    plan = _plan(n_tok)
    cos, sa, sb = tables
    tail_cos, tail_sin = jnp.ones((TAIL, LANES), F32), jnp.zeros((TAIL, LANES), F32)
    meta = jnp.broadcast_to(meta_tokens[None].astype(x.dtype), (b, N_META, d))
    h_main = x
    h_tail = jnp.concatenate([meta, jnp.zeros((b, TAIL - N_META, d), x.dtype)], axis=1)
    two = lambda g: jnp.concatenate([g, g])[None]
    for l in range(depth):
        last = l == depth - 1
        shared = (norm1_g[l][None], w_in[l], two(q_norm_g[l]), two(k_norm_g[l]), bd)
        u, qt, k, vt = _inproj(h_main, *shared, cos, sa, sb, plan.t_rows)
        ut, qtl, ktl, vtl = _inproj(h_tail, *shared, tail_cos, tail_sin, tail_sin, TAIL)

        score_bound = (math.sqrt(HEAD_DIM) * BF16_ROUNDING_SLACK
                       * jnp.max(jnp.abs(q_norm_g[l])) * jnp.max(jnp.abs(k_norm_g[l])))

        def attend(bounded, qt, qtl, k, ktl, vt, vtl, last=last):
            a_main = _attention(qt, k, ktl, vt, vtl, plan.tq, plan, bounded)
            if last:
                return (a_main,)
            return a_main, _attention(qtl, k, ktl, vt, vtl, TAIL, plan, bounded)

        attn = lax.cond(score_bound <= MAX_UNSHIFTED_SCORE,
                        functools.partial(attend, True), functools.partial(attend, False),
                        qt, qtl, k, ktl, vt, vtl)

        weights = (w_pool[l], pool_scale[l][None], w_out[l], norm2_g[l][None],
                   w_mlp_in[l], w_mlp_out[l])
        if not last:
            u_prefix = jnp.concatenate([ut[:, :N_META], u[:, :TAIL - N_META]], axis=1)
            h_tail = _post(h_tail, u_prefix, u_prefix, u, (TAIL - N_META) // HALO, attn[1],
                           *weights, TAIL, 0, seq_len, N_META)
        h_main = _post(h_main, u, ut, u, None, attn[0], *weights, plan.t_rows, N_META, seq_len,
                       plan.t_rows)
    return h_main


def kernel(x_prompt, x_sample, meta_tokens, norm1_g, w_in, q_norm_g, k_norm_g, w_pool,
           pool_scale, w_out, norm2_g, w_mlp_in, w_mlp_out):
    head = jnp.arange(LANES, dtype=jnp.int32) // HEAD_DIM
    bd = (head[:, None] == head[None, :]).astype(BF16)
    params = (meta_tokens, norm1_g, w_in.astype(BF16), q_norm_g, k_norm_g, w_pool.astype(BF16),
              pool_scale, w_out.astype(BF16), norm2_g, w_mlp_in.astype(BF16),
              w_mlp_out.astype(BF16), bd)
    tables = _rope_tables(max(x_prompt.shape[1], x_sample.shape[1]))
    return (_trunk(x_prompt, tables, *params), _trunk(x_sample, tables, *params))
```

```python
import functools
import math
from typing import NamedTuple

import jax
import jax.numpy as jnp
from jax import lax
from jax.experimental import pallas as pl
from jax.experimental.pallas import tpu as pltpu

D_MODEL = 1024
N_META = 16
GRID_W = 64
POOL_WIDTH = 512
POOL_WINDOWS = (2, 4, 8, 16)
POOL_GROUP = 128
HEAD_DIM = 64
N_HEADS = 8
N_KV_HEADS = 2
Q_PER_KV = N_HEADS // N_KV_HEADS
ATTN_WIDTH = N_HEADS * HEAD_DIM
KV_WIDTH = N_KV_HEADS * HEAD_DIM
IN_WIDTH = POOL_WIDTH + ATTN_WIDTH + 2 * KV_WIDTH
ROT_PAIRS = HEAD_DIM // 4
ROPE_THETA = 10000.0
D_FF = 4 * D_MODEL
EPS = 1e-6

LANES = 128
SUBLANES = 8
TAIL = LANES
HALO = SUBLANES
BF16_TILE_ROWS = 16
V_ROWS = HEAD_DIM + BF16_TILE_ROWS
VMEM_LIMIT = 56 * 1024 * 1024
MAX_UNSHIFTED_SCORE = 40.0
BF16_ROUNDING_SLACK = 1.02

F32 = jnp.float32
BF16 = jnp.bfloat16


def _pick_block(n, cap):
    best = LANES
    for t in range(LANES, cap + 1, LANES):
        if n % t == 0:
            best = t
    return best


class _Plan(NamedTuple):
    t_rows: int
    tq: int
    tk: int
    sub: int
    n_sub: int


def _plan(n_tok):
    sub = min(512, n_tok)
    n_sub = min(8, n_tok // sub)
    assert n_tok % (sub * n_sub) == 0, n_tok
    return _Plan(_pick_block(n_tok, 512), _pick_block(n_tok, 512), _pick_block(n_tok, 512),
                 sub, n_sub)


def _const_spec(shape):
    nd = len(shape)
    return pl.BlockSpec(shape, lambda *_: (0,) * nd, pipeline_mode=pl.Buffered(1))


def _layer_spec(stacked, layer):
    nd = stacked.ndim
    return pl.BlockSpec((None,) + stacked.shape[1:], lambda *_: (layer,) + (0,) * (nd - 1),
                        pipeline_mode=pl.Buffered(1))


def _inproj_rows(rows, h_ref, g1_ref, win_ref, qg_ref, kg_ref, bd_ref, cos_ref, sa_ref, sb_ref,
                 u_ref, qt_ref, k_ref, vt_ref):
    x = h_ref[rows, :]
    ms = jnp.mean(x * x, axis=-1, keepdims=True)
    n = (x * lax.rsqrt(ms + EPS) * g1_ref[...]).astype(BF16)
    proj = jnp.dot(n, win_ref[...], preferred_element_type=F32)
    u_ref[rows, :] = proj[:, :POOL_WIDTH]

    cos, sa, sb, bd = cos_ref[rows, :], sa_ref[rows, :], sb_ref[rows, :], bd_ref[...]

    def norm_rope(xc, g):
        sq = xc * xc
        hi = sq.astype(BF16)
        lo = (sq - hi.astype(F32)).astype(BF16)
        ss = (jnp.dot(hi, bd, preferred_element_type=F32)
              + jnp.dot(lo, bd, preferred_element_type=F32))
        y = xc * lax.rsqrt(ss * (1.0 / HEAD_DIM) + EPS) * g
        return (y * cos + pltpu.roll(y, LANES - ROT_PAIRS, 1) * sa
                + pltpu.roll(y, ROT_PAIRS, 1) * sb)

    scale = 1.0 / math.sqrt(HEAD_DIM)
    for c in range(ATTN_WIDTH // LANES):
        lo_col = POOL_WIDTH + c * LANES
        qc = norm_rope(proj[:, lo_col:lo_col + LANES], qg_ref[...]) * scale
        qt_ref[c, :, rows] = qc.T.astype(BF16)

    k0 = POOL_WIDTH + ATTN_WIDTH
    kc = norm_rope(proj[:, k0:k0 + KV_WIDTH], kg_ref[...])
    k_ref[rows, :] = kc.astype(BF16)

    vt = proj[:, k0 + KV_WIDTH:].T
    extra = V_ROWS - HEAD_DIM
    ones_row = jnp.where(lax.broadcasted_iota(jnp.int32, (extra, vt.shape[1]), 0) == 0, 1.0, 0.0)
    for g in range(N_KV_HEADS):
        vt_ref[g, :, rows] = jnp.concatenate(
            [vt[g * HEAD_DIM:(g + 1) * HEAD_DIM], ones_row], axis=0).astype(BF16)


def _inproj_kernel(h_ref, *refs):
    t = h_ref.shape[0]
    n_part = 2 if t % (2 * LANES) == 0 else 1
    for pi in range(n_part):
        _inproj_rows(slice(pi * (t // n_part), (pi + 1) * (t // n_part)), h_ref, *refs)


def _inproj(h, layer, g1, w_in, qg, kg, bd, cos, sa, sb, t_rows):
    b, rows, d = h.shape
    row_spec = lambda w: pl.BlockSpec((None, t_rows, w), lambda bi, i: (bi, i, 0))
    tab_spec = pl.BlockSpec((t_rows, LANES), lambda bi, i: (i, 0))
    return pl.pallas_call(
        _inproj_kernel,
        grid=(b, rows // t_rows),
        in_specs=[
            row_spec(d),
            _layer_spec(g1, layer), _layer_spec(w_in, layer), _layer_spec(qg, layer),
            _layer_spec(kg, layer),
            _const_spec((LANES, LANES)),
            tab_spec, tab_spec, tab_spec,
        ],
        out_specs=[
            row_spec(POOL_WIDTH),
            pl.BlockSpec((None, ATTN_WIDTH // LANES, LANES, t_rows), lambda bi, i: (bi, 0, 0, i)),
            row_spec(KV_WIDTH),
            pl.BlockSpec((None, N_KV_HEADS, V_ROWS, t_rows), lambda bi, i: (bi, 0, 0, i)),
        ],
        out_shape=[
            jax.ShapeDtypeStruct((b, rows, POOL_WIDTH), F32),
            jax.ShapeDtypeStruct((b, ATTN_WIDTH // LANES, LANES, rows), BF16),
            jax.ShapeDtypeStruct((b, rows, KV_WIDTH), BF16),
            jax.ShapeDtypeStruct((b, N_KV_HEADS, V_ROWS, rows), BF16),
        ],
        compiler_params=pltpu.CompilerParams(
            dimension_semantics=("arbitrary", "arbitrary"), vmem_limit_bytes=VMEM_LIMIT),
        name="inproj",
    )(h, g1, w_in, qg, kg, bd, cos, sa, sb)


def _stack_queries(qs_ref, qt_ref, tq):
    g = pl.program_id(1)
    qs_ref[...] = jnp.zeros_like(qs_ref)
    row0 = pl.multiple_of(g * HEAD_DIM, HEAD_DIM)
    for i in range(Q_PER_KV):
        half = slice((i % 2) * HEAD_DIM, (i % 2 + 1) * HEAD_DIM)
        qs_ref[pl.ds(row0, HEAD_DIM), i * tq:(i + 1) * tq] = qt_ref[i // 2, half, :]
    return qs_ref[...]


def _write_heads(o_ref, acc, tq):
    o = acc[:HEAD_DIM] * (1.0 / acc[HEAD_DIM:HEAD_DIM + 1])
    for a in range(Q_PER_KV // 2):
        pair = jnp.concatenate([o[:, (2 * a) * tq:(2 * a + 1) * tq],
                                o[:, (2 * a + 1) * tq:(2 * a + 2) * tq]], axis=0)
        o_ref[:, a * LANES:(a + 1) * LANES] = pair.T.astype(BF16)


def _attn_online_kernel(qt_ref, k_ref, ktl_ref, vt_ref, vtl_ref, o_ref, qs_ref, *, tk):
    tq = qt_ref.shape[2]
    nk = k_ref.shape[0] // tk
    qs = _stack_queries(qs_ref, qt_ref, tq)

    st = jnp.dot(ktl_ref[...], qs, preferred_element_type=F32)
    row = lax.broadcasted_iota(jnp.int32, st.shape, 0)
    st = jnp.where(row < N_META, st, -jnp.inf)
    m = jnp.max(st, axis=0, keepdims=True)
    p = jnp.exp(st - m).astype(BF16)
    acc = jnp.dot(vtl_ref[...], p, preferred_element_type=F32)

    def body(j, carry):
        m, acc = carry
        start = pl.multiple_of(j * tk, tk)
        st = jnp.dot(k_ref[pl.ds(start, tk), :], qs, preferred_element_type=F32)
        m_new = jnp.maximum(m, jnp.max(st, axis=0, keepdims=True))
        alpha = jnp.exp(m - m_new)
        p = jnp.exp(st - m_new).astype(BF16)
        acc = alpha * acc + jnp.dot(vt_ref[:, pl.ds(start, tk)], p, preferred_element_type=F32)
        return m_new, acc

    m, acc = lax.fori_loop(0, nk, body, (m, acc))
    _write_heads(o_ref, acc, tq)


def _attn_bounded_kernel(qt_ref, k_ref, ktl_ref, vt_ref, vtl_ref, o_ref, qs_ref, *, sub, n_sub):
    tq = qt_ref.shape[2]
    chunk = sub * n_sub
    n_chunks = k_ref.shape[0] // chunk
    qs = _stack_queries(qs_ref, qt_ref, tq)

    st = jnp.dot(ktl_ref[...], qs, preferred_element_type=F32)
    row = lax.broadcasted_iota(jnp.int32, st.shape, 0)
    p = jnp.where(row < N_META, jnp.exp(st), 0.0).astype(BF16)
    acc = jnp.dot(vtl_ref[...], p, preferred_element_type=F32)

    def body(j, acc):
        base = j * chunk
        for si in range(n_sub):
            start = pl.multiple_of(base + si * sub, LANES)
            st = jnp.dot(k_ref[pl.ds(start, sub), :], qs, preferred_element_type=F32)
            p = jnp.exp(st).astype(BF16)
            acc = acc + jnp.dot(vt_ref[:, pl.ds(start, sub)], p, preferred_element_type=F32)
        return acc

    acc = lax.fori_loop(0, n_chunks, body, acc)
    _write_heads(o_ref, acc, tq)


def _attention(qt, k, ktl, vt, vtl, tq, plan, bounded):
    b, _, _, rows = qt.shape
    n_tok = k.shape[1]
    gw = Q_PER_KV * HEAD_DIM
    if bounded:
        body = functools.partial(_attn_bounded_kernel, sub=plan.sub, n_sub=plan.n_sub)
    else:
        body = functools.partial(_attn_online_kernel, tk=plan.tk)
    return pl.pallas_call(
        body,
        grid=(b, N_KV_HEADS, rows // tq),
        in_specs=[
            pl.BlockSpec((None, Q_PER_KV // 2, LANES, tq), lambda bi, g, i: (bi, g, 0, i)),
            pl.BlockSpec((None, n_tok, KV_WIDTH), lambda bi, g, i: (bi, 0, 0)),
            pl.BlockSpec((None, TAIL, KV_WIDTH), lambda bi, g, i: (bi, 0, 0)),
            pl.BlockSpec((None, None, V_ROWS, n_tok), lambda bi, g, i: (bi, g, 0, 0)),
            pl.BlockSpec((None, None, V_ROWS, TAIL), lambda bi, g, i: (bi, g, 0, 0)),
        ],
        out_specs=pl.BlockSpec((None, tq, gw), lambda bi, g, i: (bi, i, g)),
        out_shape=jax.ShapeDtypeStruct((b, rows, ATTN_WIDTH), BF16),
        scratch_shapes=[pltpu.VMEM((KV_WIDTH, Q_PER_KV * tq), BF16)],
        compiler_params=pltpu.CompilerParams(
            dimension_semantics=("arbitrary", "arbitrary", "arbitrary"),
            vmem_limit_bytes=VMEM_LIMIT),
        name="attn_bounded" if bounded else "attn_online",
    )(qt, k, ktl, vt, vtl)


def _post_kernel(h_ref, u_ref, upa_ref, upb_ref, un_ref, a_ref, wpool_ref, pscale_ref, wo_ref,
                 g2_ref, w1_ref, w2_ref, o_ref, uext_ref, *, pos_base, seq_len, n_out, ff_chunk):
    t = h_ref.shape[0]
    step = pl.program_id(1)
    pos0 = pos_base + step * t
    r_main = lax.broadcasted_iota(jnp.int32, (t, 1), 0)
    r_halo = lax.broadcasted_iota(jnp.int32, (HALO, 1), 0)
    pos = pos0 + r_main

    u_main = u_ref[...]
    u_prev = jnp.where(step == 0, upb_ref[...], upa_ref[...])
    uext_ref[0:HALO] = jnp.where(pos0 - HALO + r_halo >= 0, u_prev, 0.0)
    uext_ref[HALO:HALO + t] = u_main
    uext_ref[HALO + t:] = jnp.where(pos0 + t + r_halo < seq_len, un_ref[...], 0.0)

    pooled = []
    for gi, w in enumerate(POOL_WINDOWS):
        half = w // 2
        lanes = slice(gi * POOL_GROUP, (gi + 1) * POOL_GROUP)
        base = HALO - half
        wsum = uext_ref[base:base + t, lanes]
        for jj in range(1, w):
            wsum = wsum + uext_ref[base + jj:base + jj + t, lanes]
        cnt = (jnp.minimum(pos + (w - half), seq_len) - jnp.maximum(pos - half, 0)).astype(F32)
        dlt = wsum / cnt - u_main[:, lanes]
        y = jnp.dot(dlt.astype(BF16), wpool_ref[gi], preferred_element_type=F32)
        pooled.append((y * pscale_ref[:, lanes]).astype(BF16))

    mix = jnp.concatenate(pooled + [a_ref[...]], axis=1)
    h1 = h_ref[...] + jnp.dot(mix, wo_ref[...], preferred_element_type=F32)

    ms = jnp.mean(h1 * h1, axis=-1, keepdims=True)
    n2 = (h1 * lax.rsqrt(ms + EPS) * g2_ref[...]).astype(BF16)
    acc = h1
    for c in range(D_FF // ff_chunk):
        cols = slice(c * ff_chunk, (c + 1) * ff_chunk)
        mid = jnp.dot(n2, w1_ref[:, cols], preferred_element_type=F32)
        act = jnp.square(jnp.maximum(mid, 0.0)).astype(BF16)
        acc = acc + jnp.dot(act, w2_ref[cols, :], preferred_element_type=F32)
    if n_out < t:
        acc = jnp.where(r_main < n_out, acc, 0.0)
    o_ref[...] = acc


def _post(h, u, u_prev_first, u_next, next_block, attn, layer, w_pool, pool_scale, w_out, g2,
          w1, w2, t_rows, pos_base, seq_len, n_out):
    b, rows, d = h.shape
    per = t_rows // HALO
    last_halo = u_next.shape[1] // HALO - 1
    row_spec = lambda w: pl.BlockSpec((None, t_rows, w), lambda bi, i: (bi, i, 0))
    halo_spec = lambda fn: pl.BlockSpec((None, HALO, POOL_WIDTH), fn)
    if next_block is None:
        next_fn = lambda bi, i: (bi, jnp.minimum((i + 1) * per, last_halo), 0)
    else:
        next_fn = lambda bi, i: (bi, next_block, 0)
    return pl.pallas_call(
        functools.partial(_post_kernel, pos_base=pos_base, seq_len=seq_len, n_out=n_out,
                          ff_chunk=1024),
        grid=(b, rows // t_rows),
        in_specs=[
            row_spec(d), row_spec(POOL_WIDTH),
            halo_spec(lambda bi, i: (bi, jnp.maximum(i * per - 1, 0), 0)),
            halo_spec(lambda bi, i: (bi, 1, 0)),
            halo_spec(next_fn),
            row_spec(ATTN_WIDTH),
            _layer_spec(w_pool, layer), _layer_spec(pool_scale, layer), _layer_spec(w_out, layer),
            _layer_spec(g2, layer), _layer_spec(w1, layer), _layer_spec(w2, layer),
        ],
        out_specs=row_spec(d),
        out_shape=jax.ShapeDtypeStruct((b, rows, d), F32),
        scratch_shapes=[pltpu.VMEM((t_rows + 2 * HALO, POOL_WIDTH), F32)],
        compiler_params=pltpu.CompilerParams(
            dimension_semantics=("arbitrary", "arbitrary"), vmem_limit_bytes=VMEM_LIMIT),
        name="post",
    )(h, u, u, u_prev_first, u_next, attn, w_pool, pool_scale, w_out, g2, w1, w2)


def _rope_tables(n_tok):
    rows = n_tok // GRID_W
    lane = jnp.arange(LANES, dtype=jnp.int32) % HEAD_DIM
    by_row = (lane // (2 * ROT_PAIRS) == 0)[None, None, :]
    first_half = ((lane % (2 * ROT_PAIRS)) // ROT_PAIRS == 0)[None, None, :]
    inv = (ROPE_THETA ** (-jnp.arange(ROT_PAIRS, dtype=F32) / ROT_PAIRS))[lane % ROT_PAIRS]
    ang_r = jnp.arange(rows, dtype=jnp.int32).astype(F32)[:, None] * inv
    ang_c = jnp.arange(GRID_W, dtype=jnp.int32).astype(F32)[:, None] * inv
    pick = lambda fr, fc: jnp.where(by_row, fr[:, None, :], fc[None, :, :])
    cos = pick(jnp.cos(ang_r), jnp.cos(ang_c))
    sin = pick(jnp.sin(ang_r), jnp.sin(ang_c))
    sa = jnp.where(first_half, -sin, 0.0)
    sb = jnp.where(first_half, 0.0, sin)
    return tuple(t.reshape(n_tok, LANES) for t in (cos, sa, sb))


def _trunk(x, tables, score_bounds, meta_tokens, norm1_g, w_in, q_gain, k_gain, w_pool,
           pool_scale, w_out, norm2_g, w_mlp_in, w_mlp_out, bd):
    b, n_tok, d = x.shape
    depth = w_in.shape[0]
    seq_len = N_META + n_tok
    plan = _plan(n_tok)
    cos, sa, sb = tables
    tail_cos, tail_sin = jnp.ones((TAIL, LANES), F32), jnp.zeros((TAIL, LANES), F32)
    meta = jnp.broadcast_to(meta_tokens[None].astype(x.dtype), (b, N_META, d))
    h_main = x
    h_tail = jnp.concatenate([meta, jnp.zeros((b, TAIL - N_META, d), x.dtype)], axis=1)
    for l in range(depth):
        last = l == depth - 1
        shared = (l, norm1_g, w_in, q_gain, k_gain, bd)
        u, qt, k, vt = _inproj(h_main, *shared, cos, sa, sb, plan.t_rows)
        ut, qtl, ktl, vtl = _inproj(h_tail, *shared, tail_cos, tail_sin, tail_sin, TAIL)

        def attend(bounded, qt, qtl, k, ktl, vt, vtl, last=last):
            a_main = _attention(qt, k, ktl, vt, vtl, plan.tq, plan, bounded)
            if last:
                return (a_main,)
            return a_main, _attention(qtl, k, ktl, vt, vtl, TAIL, plan, bounded)

        attn = lax.cond(score_bounds[l] <= MAX_UNSHIFTED_SCORE,
                        functools.partial(attend, True), functools.partial(attend, False),
                        qt, qtl, k, ktl, vt, vtl)

        weights = (l, w_pool, pool_scale, w_out, norm2_g, w_mlp_in, w_mlp_out)
        if not last:
            u_prefix = jnp.concatenate([ut[:, :N_META], u[:, :TAIL - N_META]], axis=1)
            h_tail = _post(h_tail, u_prefix, u_prefix, u, (TAIL - N_META) // HALO, attn[1],
                           *weights, TAIL, 0, seq_len, N_META)
        h_main = _post(h_main, u, ut, u, None, attn[0], *weights, plan.t_rows, N_META, seq_len,
                       plan.t_rows)
    return h_main


def kernel(x_prompt, x_sample, meta_tokens, norm1_g, w_in, q_norm_g, k_norm_g, w_pool,
           pool_scale, w_out, norm2_g, w_mlp_in, w_mlp_out):
    depth = w_in.shape[0]
    head = jnp.arange(LANES, dtype=jnp.int32) // HEAD_DIM
    bd = (head[:, None] == head[None, :]).astype(BF16)
    vec = lambda p: p.reshape(depth, 1, -1)
    two_heads = lambda g: vec(jnp.concatenate([g, g], axis=-1))
    score_bounds = (math.sqrt(HEAD_DIM) * BF16_ROUNDING_SLACK
                    * jnp.max(jnp.abs(q_norm_g), axis=-1) * jnp.max(jnp.abs(k_norm_g), axis=-1))
    params = (meta_tokens, vec(norm1_g), w_in.astype(BF16), two_heads(q_norm_g),
              two_heads(k_norm_g), w_pool.astype(BF16), vec(pool_scale), w_out.astype(BF16),
              vec(norm2_g), w_mlp_in.astype(BF16), w_mlp_out.astype(BF16), bd)
    tables = _rope_tables(max(x_prompt.shape[1], x_sample.shape[1]))
    return (_trunk(x_prompt, tables, score_bounds, *params),
            _trunk(x_sample, tables, score_bounds, *params))
```

```python
import functools
import math
from typing import NamedTuple

import jax
import jax.numpy as jnp
from jax import lax
from jax.experimental import pallas as pl
from jax.experimental.pallas import tpu as pltpu

D_MODEL = 1024
N_META = 16
GRID_W = 64
POOL_WIDTH = 512
POOL_WINDOWS = (2, 4, 8, 16)
POOL_GROUP = 128
HEAD_DIM = 64
N_HEADS = 8
N_KV_HEADS = 2
Q_PER_KV = N_HEADS // N_KV_HEADS
ATTN_WIDTH = N_HEADS * HEAD_DIM
KV_WIDTH = N_KV_HEADS * HEAD_DIM
IN_WIDTH = POOL_WIDTH + ATTN_WIDTH + 2 * KV_WIDTH
ROT_PAIRS = HEAD_DIM // 4
ROPE_THETA = 10000.0
D_FF = 4 * D_MODEL
EPS = 1e-6

LANES = 128
SUBLANES = 8
TAIL = LANES
HALO = SUBLANES
BF16_TILE_ROWS = 16
V_ROWS = HEAD_DIM + BF16_TILE_ROWS
VMEM_LIMIT = 56 * 1024 * 1024
MAX_UNSHIFTED_SCORE = 40.0
BF16_ROUNDING_SLACK = 1.02

F32 = jnp.float32
BF16 = jnp.bfloat16


def _pick_block(n, cap):
    best = LANES
    for t in range(LANES, cap + 1, LANES):
        if n % t == 0:
            best = t
    return best


class _Plan(NamedTuple):
    t_rows: int
    tq: int
    tk: int
    sub: int
    n_sub: int


def _plan(n_tok):
    sub = min(512, n_tok)
    n_sub = min(16, n_tok // sub)
    assert n_tok % (sub * n_sub) == 0, n_tok
    return _Plan(_pick_block(n_tok, 1024), _pick_block(n_tok, 512), _pick_block(n_tok, 512),
                 sub, n_sub)


def _const_spec(shape):
    nd = len(shape)
    return pl.BlockSpec(shape, lambda *_: (0,) * nd, pipeline_mode=pl.Buffered(1))


def _layer_spec(stacked, layer):
    nd = stacked.ndim
    return pl.BlockSpec((None,) + stacked.shape[1:], lambda *_: (layer,) + (0,) * (nd - 1),
                        pipeline_mode=pl.Buffered(1))


def _inproj_rows(rows, h_ref, g1_ref, win_ref, qg_ref, kg_ref, bd_ref, cos_ref, sa_ref, sb_ref,
                 u_ref, qt_ref, k_ref, vt_ref):
    x = h_ref[rows, :]
    ms = jnp.mean(x * x, axis=-1, keepdims=True)
    n = (x * lax.rsqrt(ms + EPS) * g1_ref[...]).astype(BF16)
    proj = jnp.dot(n, win_ref[...], preferred_element_type=F32)
    u_ref[rows, :] = proj[:, :POOL_WIDTH]

    cos, sa, sb, bd = cos_ref[rows, :], sa_ref[rows, :], sb_ref[rows, :], bd_ref[...]

    def norm_rope(xc, g):
        sq = xc * xc
        hi = sq.astype(BF16)
        lo = (sq - hi.astype(F32)).astype(BF16)
        ss = (jnp.dot(hi, bd, preferred_element_type=F32)
              + jnp.dot(lo, bd, preferred_element_type=F32))
        y = xc * lax.rsqrt(ss * (1.0 / HEAD_DIM) + EPS) * g
        return (y * cos + pltpu.roll(y, LANES - ROT_PAIRS, 1) * sa
                + pltpu.roll(y, ROT_PAIRS, 1) * sb)

    scale = 1.0 / math.sqrt(HEAD_DIM)
    for c in range(ATTN_WIDTH // LANES):
        lo_col = POOL_WIDTH + c * LANES
        qc = norm_rope(proj[:, lo_col:lo_col + LANES], qg_ref[...]) * scale
        qt_ref[c, :, rows] = qc.T.astype(BF16)

    k0 = POOL_WIDTH + ATTN_WIDTH
    kc = norm_rope(proj[:, k0:k0 + KV_WIDTH], kg_ref[...])
    k_ref[rows, :] = kc.astype(BF16)

    vt = proj[:, k0 + KV_WIDTH:].T
    extra = V_ROWS - HEAD_DIM
    ones_row = jnp.where(lax.broadcasted_iota(jnp.int32, (extra, vt.shape[1]), 0) == 0, 1.0, 0.0)
    for g in range(N_KV_HEADS):
        vt_ref[g, :, rows] = jnp.concatenate(
            [vt[g * HEAD_DIM:(g + 1) * HEAD_DIM], ones_row], axis=0).astype(BF16)


def _inproj_kernel(h_ref, *refs):
    t = h_ref.shape[0]
    n_part = 2 if t % (2 * LANES) == 0 else 1
    for pi in range(n_part):
        _inproj_rows(slice(pi * (t // n_part), (pi + 1) * (t // n_part)), h_ref, *refs)


def _inproj(h, layer, g1, w_in, qg, kg, bd, cos, sa, sb, t_rows):
    b, rows, d = h.shape
    row_spec = lambda w: pl.BlockSpec((None, t_rows, w), lambda bi, i: (bi, i, 0))
    tab_spec = pl.BlockSpec((t_rows, LANES), lambda bi, i: (i, 0))
    return pl.pallas_call(
        _inproj_kernel,
        grid=(b, rows // t_rows),
        in_specs=[
            row_spec(d),
            _layer_spec(g1, layer), _layer_spec(w_in, layer), _layer_spec(qg, layer),
            _layer_spec(kg, layer),
            _const_spec((LANES, LANES)),
            tab_spec, tab_spec, tab_spec,
        ],
        out_specs=[
            row_spec(POOL_WIDTH),
            pl.BlockSpec((None, ATTN_WIDTH // LANES, LANES, t_rows), lambda bi, i: (bi, 0, 0, i)),
            row_spec(KV_WIDTH),
            pl.BlockSpec((None, N_KV_HEADS, V_ROWS, t_rows), lambda bi, i: (bi, 0, 0, i)),
        ],
        out_shape=[
            jax.ShapeDtypeStruct((b, rows, POOL_WIDTH), F32),
            jax.ShapeDtypeStruct((b, ATTN_WIDTH // LANES, LANES, rows), BF16),
            jax.ShapeDtypeStruct((b, rows, KV_WIDTH), BF16),
            jax.ShapeDtypeStruct((b, N_KV_HEADS, V_ROWS, rows), BF16),
        ],
        compiler_params=pltpu.CompilerParams(
            dimension_semantics=("arbitrary", "arbitrary"), vmem_limit_bytes=VMEM_LIMIT),
        name="inproj",
    )(h, g1, w_in, qg, kg, bd, cos, sa, sb)


def _stack_queries(qs_ref, qt_ref, tq):
    g = pl.program_id(1)
    qs_ref[...] = jnp.zeros_like(qs_ref)
    row0 = pl.multiple_of(g * HEAD_DIM, HEAD_DIM)
    for i in range(Q_PER_KV):
        half = slice((i % 2) * HEAD_DIM, (i % 2 + 1) * HEAD_DIM)
        qs_ref[pl.ds(row0, HEAD_DIM), i * tq:(i + 1) * tq] = qt_ref[i // 2, half, :]
    return qs_ref[...]


def _write_heads(o_ref, acc, tq):
    o = acc[:HEAD_DIM] * (1.0 / acc[HEAD_DIM:HEAD_DIM + 1])
    for a in range(Q_PER_KV // 2):
        pair = jnp.concatenate([o[:, (2 * a) * tq:(2 * a + 1) * tq],
                                o[:, (2 * a + 1) * tq:(2 * a + 2) * tq]], axis=0)
        o_ref[:, a * LANES:(a + 1) * LANES] = pair.T.astype(BF16)


def _attn_online_kernel(qt_ref, k_ref, ktl_ref, vt_ref, vtl_ref, o_ref, qs_ref, *, tk):
    tq = qt_ref.shape[2]
    nk = k_ref.shape[0] // tk
    qs = _stack_queries(qs_ref, qt_ref, tq)

    st = jnp.dot(ktl_ref[...], qs, preferred_element_type=F32)
    row = lax.broadcasted_iota(jnp.int32, st.shape, 0)
    st = jnp.where(row < N_META, st, -jnp.inf)
    m = jnp.max(st, axis=0, keepdims=True)
    p = jnp.exp(st - m).astype(BF16)
    acc = jnp.dot(vtl_ref[...], p, preferred_element_type=F32)

    def body(j, carry):
        m, acc = carry
        start = pl.multiple_of(j * tk, tk)
        st = jnp.dot(k_ref[pl.ds(start, tk), :], qs, preferred_element_type=F32)
        m_new = jnp.maximum(m, jnp.max(st, axis=0, keepdims=True))
        alpha = jnp.exp(m - m_new)
        p = jnp.exp(st - m_new).astype(BF16)
        acc = alpha * acc + jnp.dot(vt_ref[:, pl.ds(start, tk)], p, preferred_element_type=F32)
        return m_new, acc

    m, acc = lax.fori_loop(0, nk, body, (m, acc))
    _write_heads(o_ref, acc, tq)


def _attn_bounded_kernel(qt_ref, k_ref, ktl_ref, vt_ref, vtl_ref, o_ref, qs_ref, *, sub, n_sub):
    tq = qt_ref.shape[2]
    chunk = sub * n_sub
    n_chunks = k_ref.shape[0] // chunk
    qs = _stack_queries(qs_ref, qt_ref, tq)

    st = jnp.dot(ktl_ref[...], qs, preferred_element_type=F32)
    row = lax.broadcasted_iota(jnp.int32, st.shape, 0)
    p = jnp.where(row < N_META, jnp.exp(st), 0.0).astype(BF16)
    acc = jnp.dot(vtl_ref[...], p, preferred_element_type=F32)

    def body(j, acc):
        base = j * chunk
        for si in range(n_sub):
            start = pl.multiple_of(base + si * sub, LANES)
            st = jnp.dot(k_ref[pl.ds(start, sub), :], qs, preferred_element_type=F32)
            p = jnp.exp(st).astype(BF16)
            acc = acc + jnp.dot(vt_ref[:, pl.ds(start, sub)], p, preferred_element_type=F32)
        return acc

    acc = lax.fori_loop(0, n_chunks, body, acc)
    _write_heads(o_ref, acc, tq)


def _attention(qt, k, ktl, vt, vtl, tq, plan, bounded):
    b, _, _, rows = qt.shape
    n_tok = k.shape[1]
    gw = Q_PER_KV * HEAD_DIM
    if bounded:
        body = functools.partial(_attn_bounded_kernel, sub=plan.sub, n_sub=plan.n_sub)
    else:
        body = functools.partial(_attn_online_kernel, tk=plan.tk)
    return pl.pallas_call(
        body,
        grid=(b, N_KV_HEADS, rows // tq),
        in_specs=[
            pl.BlockSpec((None, Q_PER_KV // 2, LANES, tq), lambda bi, g, i: (bi, g, 0, i)),
            pl.BlockSpec((None, n_tok, KV_WIDTH), lambda bi, g, i: (bi, 0, 0)),
            pl.BlockSpec((None, TAIL, KV_WIDTH), lambda bi, g, i: (bi, 0, 0)),
            pl.BlockSpec((None, None, V_ROWS, n_tok), lambda bi, g, i: (bi, g, 0, 0)),
            pl.BlockSpec((None, None, V_ROWS, TAIL), lambda bi, g, i: (bi, g, 0, 0)),
        ],
        out_specs=pl.BlockSpec((None, tq, gw), lambda bi, g, i: (bi, i, g)),
        out_shape=jax.ShapeDtypeStruct((b, rows, ATTN_WIDTH), BF16),
        scratch_shapes=[pltpu.VMEM((KV_WIDTH, Q_PER_KV * tq), BF16)],
        compiler_params=pltpu.CompilerParams(
            dimension_semantics=("arbitrary", "arbitrary", "arbitrary"),
            vmem_limit_bytes=VMEM_LIMIT),
        name="attn_bounded" if bounded else "attn_online",
    )(qt, k, ktl, vt, vtl)


def _post_kernel(h_ref, u_ref, upa_ref, upb_ref, un_ref, a_ref, wpool_ref, pscale_ref, wo_ref,
                 g2_ref, w1_ref, w2_ref, o_ref, uext_ref, *, pos_base, seq_len, n_out, ff_chunk):
    t = h_ref.shape[0]
    step = pl.program_id(1)
    pos0 = pos_base + step * t
    r_main = lax.broadcasted_iota(jnp.int32, (t, 1), 0)
    r_halo = lax.broadcasted_iota(jnp.int32, (HALO, 1), 0)
    pos = pos0 + r_main

    u_main = u_ref[...]
    u_prev = jnp.where(step == 0, upb_ref[...], upa_ref[...])
    uext_ref[0:HALO] = jnp.where(pos0 - HALO + r_halo >= 0, u_prev, 0.0)
    uext_ref[HALO:HALO + t] = u_main
    uext_ref[HALO + t:] = jnp.where(pos0 + t + r_halo < seq_len, un_ref[...], 0.0)

    pooled = []
    for gi, w in enumerate(POOL_WINDOWS):
        half = w // 2
        lanes = slice(gi * POOL_GROUP, (gi + 1) * POOL_GROUP)
        base = HALO - half
        wsum = uext_ref[base:base + t, lanes]
        for jj in range(1, w):
            wsum = wsum + uext_ref[base + jj:base + jj + t, lanes]
        cnt = (jnp.minimum(pos + (w - half), seq_len) - jnp.maximum(pos - half, 0)).astype(F32)
        dlt = wsum / cnt - u_main[:, lanes]
        y = jnp.dot(dlt.astype(BF16), wpool_ref[gi], preferred_element_type=F32)
        pooled.append((y * pscale_ref[:, lanes]).astype(BF16))

    mix = jnp.concatenate(pooled + [a_ref[...]], axis=1)
    h1 = h_ref[...] + jnp.dot(mix, wo_ref[...], preferred_element_type=F32)

    ms = jnp.mean(h1 * h1, axis=-1, keepdims=True)
    n2 = (h1 * lax.rsqrt(ms + EPS) * g2_ref[...]).astype(BF16)
    acc = h1
    for c in range(D_FF // ff_chunk):
        cols = slice(c * ff_chunk, (c + 1) * ff_chunk)
        mid = jnp.dot(n2, w1_ref[:, cols], preferred_element_type=F32)
        act = jnp.square(jnp.maximum(mid, 0.0)).astype(BF16)
        acc = acc + jnp.dot(act, w2_ref[cols, :], preferred_element_type=F32)
    if n_out < t:
        acc = jnp.where(r_main < n_out, acc, 0.0)
    o_ref[...] = acc


def _post(h, u, u_prev_first, u_next, next_block, attn, layer, w_pool, pool_scale, w_out, g2,
          w1, w2, t_rows, pos_base, seq_len, n_out):
    b, rows, d = h.shape
    per = t_rows // HALO
    last_halo = u_next.shape[1] // HALO - 1
    row_spec = lambda w: pl.BlockSpec((None, t_rows, w), lambda bi, i: (bi, i, 0))
    halo_spec = lambda fn: pl.BlockSpec((None, HALO, POOL_WIDTH), fn)
    if next_block is None:
        next_fn = lambda bi, i: (bi, jnp.minimum((i + 1) * per, last_halo), 0)
    else:
        next_fn = lambda bi, i: (bi, next_block, 0)
    return pl.pallas_call(
        functools.partial(_post_kernel, pos_base=pos_base, seq_len=seq_len, n_out=n_out,
                          ff_chunk=512),
        grid=(b, rows // t_rows),
        in_specs=[
            row_spec(d), row_spec(POOL_WIDTH),
            halo_spec(lambda bi, i: (bi, jnp.maximum(i * per - 1, 0), 0)),
            halo_spec(lambda bi, i: (bi, 1, 0)),
            halo_spec(next_fn),
            row_spec(ATTN_WIDTH),
            _layer_spec(w_pool, layer), _layer_spec(pool_scale, layer), _layer_spec(w_out, layer),
            _layer_spec(g2, layer), _layer_spec(w1, layer), _layer_spec(w2, layer),
        ],
        out_specs=row_spec(d),
        out_shape=jax.ShapeDtypeStruct((b, rows, d), F32),
        scratch_shapes=[pltpu.VMEM((t_rows + 2 * HALO, POOL_WIDTH), F32)],
        compiler_params=pltpu.CompilerParams(
            dimension_semantics=("arbitrary", "arbitrary"), vmem_limit_bytes=VMEM_LIMIT),
        name="post",
    )(h, u, u, u_prev_first, u_next, attn, w_pool, pool_scale, w_out, g2, w1, w2)


def _rope_tables(n_tok):
    rows = n_tok // GRID_W
    lane = jnp.arange(LANES, dtype=jnp.int32) % HEAD_DIM
    by_row = (lane // (2 * ROT_PAIRS) == 0)[None, None, :]
    first_half = ((lane % (2 * ROT_PAIRS)) // ROT_PAIRS == 0)[None, None, :]
    inv = (ROPE_THETA ** (-jnp.arange(ROT_PAIRS, dtype=F32) / ROT_PAIRS))[lane % ROT_PAIRS]
    ang_r = jnp.arange(rows, dtype=jnp.int32).astype(F32)[:, None] * inv
    ang_c = jnp.arange(GRID_W, dtype=jnp.int32).astype(F32)[:, None] * inv
    pick = lambda fr, fc: jnp.where(by_row, fr[:, None, :], fc[None, :, :])
    cos = pick(jnp.cos(ang_r), jnp.cos(ang_c))
    sin = pick(jnp.sin(ang_r), jnp.sin(ang_c))
    sa = jnp.where(first_half, -sin, 0.0)
    sb = jnp.where(first_half, 0.0, sin)
    return tuple(t.reshape(n_tok, LANES) for t in (cos, sa, sb))


def _trunk(x, tables, score_bounds, meta_tokens, norm1_g, w_in, q_gain, k_gain, w_pool,
           pool_scale, w_out, norm2_g, w_mlp_in, w_mlp_out, bd):
    b, n_tok, d = x.shape
    depth = w_in.shape[0]
    seq_len = N_META + n_tok
    plan = _plan(n_tok)
    cos, sa, sb = tables
    tail_cos, tail_sin = jnp.ones((TAIL, LANES), F32), jnp.zeros((TAIL, LANES), F32)
    meta = jnp.broadcast_to(meta_tokens[None].astype(x.dtype), (b, N_META, d))
    h_main = x
    h_tail = jnp.concatenate([meta, jnp.zeros((b, TAIL - N_META, d), x.dtype)], axis=1)
    for l in range(depth):
        last = l == depth - 1
        shared = (l, norm1_g, w_in, q_gain, k_gain, bd)
        u, qt, k, vt = _inproj(h_main, *shared, cos, sa, sb, plan.t_rows)
        ut, qtl, ktl, vtl = _inproj(h_tail, *shared, tail_cos, tail_sin, tail_sin, TAIL)

        def attend(bounded, qt, qtl, k, ktl, vt, vtl, last=last):
            a_main = _attention(qt, k, ktl, vt, vtl, plan.tq, plan, bounded)
            if last:
                return (a_main,)
            return a_main, _attention(qtl, k, ktl, vt, vtl, TAIL, plan, bounded)

        attn = lax.cond(score_bounds[l] <= MAX_UNSHIFTED_SCORE,
                        functools.partial(attend, True), functools.partial(attend, False),
                        qt, qtl, k, ktl, vt, vtl)

        weights = (l, w_pool, pool_scale, w_out, norm2_g, w_mlp_in, w_mlp_out)
        if not last:
            u_prefix = jnp.concatenate([ut[:, :N_META], u[:, :TAIL - N_META]], axis=1)
            h_tail = _post(h_tail, u_prefix, u_prefix, u, (TAIL - N_META) // HALO, attn[1],
                           *weights, TAIL, 0, seq_len, N_META)
        h_main = _post(h_main, u, ut, u, None, attn[0], *weights, plan.t_rows, N_META, seq_len,
                       plan.t_rows)
    return h_main


def kernel(x_prompt, x_sample, meta_tokens, norm1_g, w_in, q_norm_g, k_norm_g, w_pool,
           pool_scale, w_out, norm2_g, w_mlp_in, w_mlp_out):
    depth = w_in.shape[0]
    head = jnp.arange(LANES, dtype=jnp.int32) // HEAD_DIM
    bd = (head[:, None] == head[None, :]).astype(BF16)
    vec = lambda p: p.reshape(depth, 1, -1)
    two_heads = lambda g: vec(jnp.concatenate([g, g], axis=-1))
    score_bounds = (math.sqrt(HEAD_DIM) * BF16_ROUNDING_SLACK
                    * jnp.max(jnp.abs(q_norm_g), axis=-1) * jnp.max(jnp.abs(k_norm_g), axis=-1))
    params = (meta_tokens, vec(norm1_g), w_in.astype(BF16), two_heads(q_norm_g),
              two_heads(k_norm_g), w_pool.astype(BF16), vec(pool_scale), w_out.astype(BF16),
              vec(norm2_g), w_mlp_in.astype(BF16), w_mlp_out.astype(BF16), bd)
    tables = _rope_tables(max(x_prompt.shape[1], x_sample.shape[1]))
    return (_trunk(x_prompt, tables, score_bounds, *params),
            _trunk(x_sample, tables, score_bounds, *params))
```

```python
import functools
import math
from typing import NamedTuple

import jax
import jax.numpy as jnp
from jax import lax
from jax.experimental import pallas as pl
from jax.experimental.pallas import tpu as pltpu

D_MODEL = 1024
N_META = 16
GRID_W = 64
POOL_WIDTH = 512
POOL_WINDOWS = (2, 4, 8, 16)
POOL_GROUP = 128
HEAD_DIM = 64
N_HEADS = 8
N_KV_HEADS = 2
Q_PER_KV = N_HEADS // N_KV_HEADS
ATTN_WIDTH = N_HEADS * HEAD_DIM
KV_WIDTH = N_KV_HEADS * HEAD_DIM
IN_WIDTH = POOL_WIDTH + ATTN_WIDTH + 2 * KV_WIDTH
ROT_PAIRS = HEAD_DIM // 4
ROPE_THETA = 10000.0
D_FF = 4 * D_MODEL
EPS = 1e-6

LANES = 128
SUBLANES = 8
TAIL = LANES
HALO = SUBLANES
BF16_TILE_ROWS = 16
V_ROWS = HEAD_DIM + BF16_TILE_ROWS
VMEM_LIMIT = 56 * 1024 * 1024
MAX_UNSHIFTED_SCORE = 40.0
BF16_ROUNDING_SLACK = 1.02
SCORES_PER_STEP = 1 << 24

F32 = jnp.float32
BF16 = jnp.bfloat16


def _pick_block(n, cap):
    best = LANES
    for t in range(LANES, cap + 1, LANES):
        if n % t == 0:
            best = t
    return best


class _Plan(NamedTuple):
    t_rows: int
    tq: int
    tk: int
    sub: int
    n_sub: int


def _plan(n_tok):
    sub = min(512, n_tok)
    n_sub = min(16, n_tok // sub)
    assert n_tok % (sub * n_sub) == 0, n_tok
    tq = _pick_block(n_tok, min(1024, SCORES_PER_STEP // (Q_PER_KV * n_tok)))
    return _Plan(_pick_block(n_tok, 1024), tq, _pick_block(n_tok, 512), sub, n_sub)


def _const_spec(shape):
    nd = len(shape)
    return pl.BlockSpec(shape, lambda *_: (0,) * nd, pipeline_mode=pl.Buffered(1))


def _layer_spec(stacked, layer):
    nd = stacked.ndim
    return pl.BlockSpec((None,) + stacked.shape[1:], lambda *_: (layer,) + (0,) * (nd - 1),
                        pipeline_mode=pl.Buffered(1))


def _inproj_rows(rows, h_ref, g1_ref, win_ref, qg_ref, kg_ref, bd_ref, cos_ref, sa_ref, sb_ref,
                 u_ref, qt_ref, k_ref, vt_ref):
    x = h_ref[rows, :]
    ms = jnp.mean(x * x, axis=-1, keepdims=True)
    n = (x * lax.rsqrt(ms + EPS) * g1_ref[...]).astype(BF16)
    proj = jnp.dot(n, win_ref[...], preferred_element_type=F32)
    u_ref[rows, :] = proj[:, :POOL_WIDTH]

    cos, sa, sb, bd = cos_ref[rows, :], sa_ref[rows, :], sb_ref[rows, :], bd_ref[...]

    def norm_rope(xc, g):
        sq = xc * xc
        hi = sq.astype(BF16)
        lo = (sq - hi.astype(F32)).astype(BF16)
        ss = (jnp.dot(hi, bd, preferred_element_type=F32)
              + jnp.dot(lo, bd, preferred_element_type=F32))
        y = xc * lax.rsqrt(ss * (1.0 / HEAD_DIM) + EPS) * g
        return (y * cos + pltpu.roll(y, LANES - ROT_PAIRS, 1) * sa
                + pltpu.roll(y, ROT_PAIRS, 1) * sb)

    scale = 1.0 / math.sqrt(HEAD_DIM)
    for c in range(ATTN_WIDTH // LANES):
        lo_col = POOL_WIDTH + c * LANES
        qc = norm_rope(proj[:, lo_col:lo_col + LANES], qg_ref[...]) * scale
        qt_ref[c, :, rows] = qc.T.astype(BF16)

    k0 = POOL_WIDTH + ATTN_WIDTH
    kc = norm_rope(proj[:, k0:k0 + KV_WIDTH], kg_ref[...])
    k_ref[rows, :] = kc.astype(BF16)

    vt = proj[:, k0 + KV_WIDTH:].T
    extra = V_ROWS - HEAD_DIM
    ones_row = jnp.where(lax.broadcasted_iota(jnp.int32, (extra, vt.shape[1]), 0) == 0, 1.0, 0.0)
    for g in range(N_KV_HEADS):
        vt_ref[g, :, rows] = jnp.concatenate(
            [vt[g * HEAD_DIM:(g + 1) * HEAD_DIM], ones_row], axis=0).astype(BF16)


def _inproj_kernel(h_ref, *refs):
    t = h_ref.shape[0]
    n_part = 2 if t % (2 * LANES) == 0 else 1
    for pi in range(n_part):
        _inproj_rows(slice(pi * (t // n_part), (pi + 1) * (t // n_part)), h_ref, *refs)


def _inproj(h, layer, g1, w_in, qg, kg, bd, cos, sa, sb, t_rows):
    b, rows, d = h.shape
    row_spec = lambda w: pl.BlockSpec((None, t_rows, w), lambda bi, i: (bi, i, 0))
    tab_spec = pl.BlockSpec((t_rows, LANES), lambda bi, i: (i, 0))
    return pl.pallas_call(
        _inproj_kernel,
        grid=(b, rows // t_rows),
        in_specs=[
            row_spec(d),
            _layer_spec(g1, layer), _layer_spec(w_in, layer), _layer_spec(qg, layer),
            _layer_spec(kg, layer),
            _const_spec((LANES, LANES)),
            tab_spec, tab_spec, tab_spec,
        ],
        out_specs=[
            row_spec(POOL_WIDTH),
            pl.BlockSpec((None, ATTN_WIDTH // LANES, LANES, t_rows), lambda bi, i: (bi, 0, 0, i)),
            row_spec(KV_WIDTH),
            pl.BlockSpec((None, N_KV_HEADS, V_ROWS, t_rows), lambda bi, i: (bi, 0, 0, i)),
        ],
        out_shape=[
            jax.ShapeDtypeStruct((b, rows, POOL_WIDTH), F32),
            jax.ShapeDtypeStruct((b, ATTN_WIDTH // LANES, LANES, rows), BF16),
            jax.ShapeDtypeStruct((b, rows, KV_WIDTH), BF16),
            jax.ShapeDtypeStruct((b, N_KV_HEADS, V_ROWS, rows), BF16),
        ],
        compiler_params=pltpu.CompilerParams(
            dimension_semantics=("arbitrary", "arbitrary"), vmem_limit_bytes=VMEM_LIMIT),
        name="inproj",
    )(h, g1, w_in, qg, kg, bd, cos, sa, sb)


def _stack_queries(qs_ref, qt_ref, tq):
    g = pl.program_id(1)
    qs_ref[...] = jnp.zeros_like(qs_ref)
    row0 = pl.multiple_of(g * HEAD_DIM, HEAD_DIM)
    for i in range(Q_PER_KV):
        half = slice((i % 2) * HEAD_DIM, (i % 2 + 1) * HEAD_DIM)
        qs_ref[pl.ds(row0, HEAD_DIM), i * tq:(i + 1) * tq] = qt_ref[i // 2, half, :]
    return qs_ref[...]


def _write_heads(o_ref, acc, tq):
    o = acc[:HEAD_DIM] * (1.0 / acc[HEAD_DIM:HEAD_DIM + 1])
    for a in range(Q_PER_KV // 2):
        pair = jnp.concatenate([o[:, (2 * a) * tq:(2 * a + 1) * tq],
                                o[:, (2 * a + 1) * tq:(2 * a + 2) * tq]], axis=0)
        o_ref[:, a * LANES:(a + 1) * LANES] = pair.T.astype(BF16)


def _attn_online_kernel(qt_ref, k_ref, ktl_ref, vt_ref, vtl_ref, o_ref, qs_ref, *, tk):
    tq = qt_ref.shape[2]
    nk = k_ref.shape[0] // tk
    qs = _stack_queries(qs_ref, qt_ref, tq)

    st = jnp.dot(ktl_ref[:N_META, :], qs, preferred_element_type=F32)
    m = jnp.max(st, axis=0, keepdims=True)
    p = jnp.exp(st - m).astype(BF16)
    acc = jnp.dot(vtl_ref[:, :N_META], p, preferred_element_type=F32)

    def body(j, carry):
        m, acc = carry
        start = pl.multiple_of(j * tk, tk)
        st = jnp.dot(k_ref[pl.ds(start, tk), :], qs, preferred_element_type=F32)
        m_new = jnp.maximum(m, jnp.max(st, axis=0, keepdims=True))
        alpha = jnp.exp(m - m_new)
        p = jnp.exp(st - m_new).astype(BF16)
        acc = alpha * acc + jnp.dot(vt_ref[:, pl.ds(start, tk)], p, preferred_element_type=F32)
        return m_new, acc

    m, acc = lax.fori_loop(0, nk, body, (m, acc))
    _write_heads(o_ref, acc, tq)


def _attn_bounded_kernel(qt_ref, k_ref, ktl_ref, vt_ref, vtl_ref, o_ref, qs_ref, *, sub, n_sub):
    tq = qt_ref.shape[2]
    chunk = sub * n_sub
    n_chunks = k_ref.shape[0] // chunk
    qs = _stack_queries(qs_ref, qt_ref, tq)

    st = jnp.dot(ktl_ref[:N_META, :], qs, preferred_element_type=F32)
    acc = jnp.dot(vtl_ref[:, :N_META], jnp.exp(st).astype(BF16), preferred_element_type=F32)

    def body(j, acc):
        base = j * chunk
        for si in range(n_sub):
            start = pl.multiple_of(base + si * sub, LANES)
            st = jnp.dot(k_ref[pl.ds(start, sub), :], qs, preferred_element_type=F32)
            p = jnp.exp(st).astype(BF16)
            acc = acc + jnp.dot(vt_ref[:, pl.ds(start, sub)], p, preferred_element_type=F32)
        return acc

    acc = lax.fori_loop(0, n_chunks, body, acc)
    _write_heads(o_ref, acc, tq)


def _attention(qt, k, ktl, vt, vtl, tq, plan, bounded):
    b, _, _, rows = qt.shape
    n_tok = k.shape[1]
    gw = Q_PER_KV * HEAD_DIM
    if bounded:
        body = functools.partial(_attn_bounded_kernel, sub=plan.sub, n_sub=plan.n_sub)
    else:
        body = functools.partial(_attn_online_kernel, tk=plan.tk)
    return pl.pallas_call(
        body,
        grid=(b, N_KV_HEADS, rows // tq),
        in_specs=[
            pl.BlockSpec((None, Q_PER_KV // 2, LANES, tq), lambda bi, g, i: (bi, g, 0, i)),
            pl.BlockSpec((None, n_tok, KV_WIDTH), lambda bi, g, i: (bi, 0, 0)),
            pl.BlockSpec((None, TAIL, KV_WIDTH), lambda bi, g, i: (bi, 0, 0)),
            pl.BlockSpec((None, None, V_ROWS, n_tok), lambda bi, g, i: (bi, g, 0, 0)),
            pl.BlockSpec((None, None, V_ROWS, TAIL), lambda bi, g, i: (bi, g, 0, 0)),
        ],
        out_specs=pl.BlockSpec((None, tq, gw), lambda bi, g, i: (bi, i, g)),
        out_shape=jax.ShapeDtypeStruct((b, rows, ATTN_WIDTH), BF16),
        scratch_shapes=[pltpu.VMEM((KV_WIDTH, Q_PER_KV * tq), BF16)],
        compiler_params=pltpu.CompilerParams(
            dimension_semantics=("arbitrary", "arbitrary", "arbitrary"),
            vmem_limit_bytes=VMEM_LIMIT),
        name="attn_bounded" if bounded else "attn_online",
    )(qt, k, ktl, vt, vtl)


def _post_kernel(h_ref, u_ref, upa_ref, upb_ref, un_ref, a_ref, wpool_ref, pscale_ref, wo_ref,
                 g2_ref, w1_ref, w2_ref, o_ref, uext_ref, *, pos_base, seq_len, n_out, ff_chunk):
    t = h_ref.shape[0]
    step = pl.program_id(1)
    pos0 = pos_base + step * t
    r_main = lax.broadcasted_iota(jnp.int32, (t, 1), 0)
    r_halo = lax.broadcasted_iota(jnp.int32, (HALO, 1), 0)
    pos = pos0 + r_main

    u_main = u_ref[...]
    u_prev = jnp.where(step == 0, upb_ref[...], upa_ref[...])
    uext_ref[0:HALO] = jnp.where(pos0 - HALO + r_halo >= 0, u_prev, 0.0)
    uext_ref[HALO:HALO + t] = u_main
    uext_ref[HALO + t:] = jnp.where(pos0 + t + r_halo < seq_len, un_ref[...], 0.0)

    pooled = []
    for gi, w in enumerate(POOL_WINDOWS):
        half = w // 2
        lanes = slice(gi * POOL_GROUP, (gi + 1) * POOL_GROUP)
        base = HALO - half
        wsum = uext_ref[base:base + t, lanes]
        for jj in range(1, w):
            wsum = wsum + uext_ref[base + jj:base + jj + t, lanes]
        cnt = (jnp.minimum(pos + (w - half), seq_len) - jnp.maximum(pos - half, 0)).astype(F32)
        dlt = wsum / cnt - u_main[:, lanes]
        y = jnp.dot(dlt.astype(BF16), wpool_ref[gi], preferred_element_type=F32)
        pooled.append((y * pscale_ref[:, lanes]).astype(BF16))

    mix = jnp.concatenate(pooled + [a_ref[...]], axis=1)
    h1 = h_ref[...] + jnp.dot(mix, wo_ref[...], preferred_element_type=F32)

    ms = jnp.mean(h1 * h1, axis=-1, keepdims=True)
    n2 = (h1 * lax.rsqrt(ms + EPS) * g2_ref[...]).astype(BF16)
    acc = h1
    for c in range(D_FF // ff_chunk):
        cols = slice(c * ff_chunk, (c + 1) * ff_chunk)
        mid = jnp.dot(n2, w1_ref[:, cols], preferred_element_type=F32)
        act = jnp.square(jnp.maximum(mid, 0.0)).astype(BF16)
        acc = acc + jnp.dot(act, w2_ref[cols, :], preferred_element_type=F32)
    if n_out < t:
        acc = jnp.where(r_main < n_out, acc, 0.0)
    o_ref[...] = acc


def _post(h, u, u_prev_first, u_next, next_block, attn, layer, w_pool, pool_scale, w_out, g2,
          w1, w2, t_rows, pos_base, seq_len, n_out):
    b, rows, d = h.shape
    per = t_rows // HALO
    last_halo = u_next.shape[1] // HALO - 1
    row_spec = lambda w: pl.BlockSpec((None, t_rows, w), lambda bi, i: (bi, i, 0))
    halo_spec = lambda fn: pl.BlockSpec((None, HALO, POOL_WIDTH), fn)
    if next_block is None:
        next_fn = lambda bi, i: (bi, jnp.minimum((i + 1) * per, last_halo), 0)
    else:
        next_fn = lambda bi, i: (bi, next_block, 0)
    return pl.pallas_call(
        functools.partial(_post_kernel, pos_base=pos_base, seq_len=seq_len, n_out=n_out,
                          ff_chunk=512),
        grid=(b, rows // t_rows),
        in_specs=[
            row_spec(d), row_spec(POOL_WIDTH),
            halo_spec(lambda bi, i: (bi, jnp.maximum(i * per - 1, 0), 0)),
            halo_spec(lambda bi, i: (bi, 1, 0)),
            halo_spec(next_fn),
            row_spec(ATTN_WIDTH),
            _layer_spec(w_pool, layer), _layer_spec(pool_scale, layer), _layer_spec(w_out, layer),
            _layer_spec(g2, layer), _layer_spec(w1, layer), _layer_spec(w2, layer),
        ],
        out_specs=row_spec(d),
        out_shape=jax.ShapeDtypeStruct((b, rows, d), F32),
        scratch_shapes=[pltpu.VMEM((t_rows + 2 * HALO, POOL_WIDTH), F32)],
        compiler_params=pltpu.CompilerParams(
            dimension_semantics=("arbitrary", "arbitrary"), vmem_limit_bytes=VMEM_LIMIT),
        name="post",
    )(h, u, u, u_prev_first, u_next, attn, w_pool, pool_scale, w_out, g2, w1, w2)


def _rope_tables(n_tok):
    rows = n_tok // GRID_W
    lane = jnp.arange(LANES, dtype=jnp.int32) % HEAD_DIM
    by_row = (lane // (2 * ROT_PAIRS) == 0)[None, None, :]
    first_half = ((lane % (2 * ROT_PAIRS)) // ROT_PAIRS == 0)[None, None, :]
    inv = (ROPE_THETA ** (-jnp.arange(ROT_PAIRS, dtype=F32) / ROT_PAIRS))[lane % ROT_PAIRS]
    ang_r = jnp.arange(rows, dtype=jnp.int32).astype(F32)[:, None] * inv
    ang_c = jnp.arange(GRID_W, dtype=jnp.int32).astype(F32)[:, None] * inv
    pick = lambda fr, fc: jnp.where(by_row, fr[:, None, :], fc[None, :, :])
    cos = pick(jnp.cos(ang_r), jnp.cos(ang_c))
    sin = pick(jnp.sin(ang_r), jnp.sin(ang_c))
    sa = jnp.where(first_half, -sin, 0.0)
    sb = jnp.where(first_half, 0.0, sin)
    return tuple(t.reshape(n_tok, LANES) for t in (cos, sa, sb))


def _trunk(x, tables, score_bounds, meta_tokens, norm1_g, w_in, q_gain, k_gain, w_pool,
           pool_scale, w_out, norm2_g, w_mlp_in, w_mlp_out, bd):
    b, n_tok, d = x.shape
    depth = w_in.shape[0]
    seq_len = N_META + n_tok
    plan = _plan(n_tok)
    cos, sa, sb = tables
    tail_cos, tail_sin = jnp.ones((TAIL, LANES), F32), jnp.zeros((TAIL, LANES), F32)
    meta = jnp.broadcast_to(meta_tokens[None].astype(x.dtype), (b, N_META, d))
    h_main = x
    h_tail = jnp.concatenate([meta, jnp.zeros((b, TAIL - N_META, d), x.dtype)], axis=1)
    for l in range(depth):
        last = l == depth - 1
        shared = (l, norm1_g, w_in, q_gain, k_gain, bd)
        u, qt, k, vt = _inproj(h_main, *shared, cos, sa, sb, plan.t_rows)
        ut, qtl, ktl, vtl = _inproj(h_tail, *shared, tail_cos, tail_sin, tail_sin, TAIL)

        def attend(bounded, qt, qtl, k, ktl, vt, vtl, last=last):
            a_main = _attention(qt, k, ktl, vt, vtl, plan.tq, plan, bounded)
            if last:
                return (a_main,)
            return a_main, _attention(qtl, k, ktl, vt, vtl, TAIL, plan, bounded)

        attn = lax.cond(score_bounds[l] <= MAX_UNSHIFTED_SCORE,
                        functools.partial(attend, True), functools.partial(attend, False),
                        qt, qtl, k, ktl, vt, vtl)

        weights = (l, w_pool, pool_scale, w_out, norm2_g, w_mlp_in, w_mlp_out)
        if not last:
            u_prefix = jnp.concatenate([ut[:, :N_META], u[:, :TAIL - N_META]], axis=1)
            h_tail = _post(h_tail, u_prefix, u_prefix, u, (TAIL - N_META) // HALO, attn[1],
                           *weights, TAIL, 0, seq_len, N_META)
        h_main = _post(h_main, u, ut, u, None, attn[0], *weights, plan.t_rows, N_META, seq_len,
                       plan.t_rows)
    return h_main


def kernel(x_prompt, x_sample, meta_tokens, norm1_g, w_in, q_norm_g, k_norm_g, w_pool,
           pool_scale, w_out, norm2_g, w_mlp_in, w_mlp_out):
    depth = w_in.shape[0]
    head = jnp.arange(LANES, dtype=jnp.int32) // HEAD_DIM
    bd = (head[:, None] == head[None, :]).astype(BF16)
    vec = lambda p: p.reshape(depth, 1, -1)
    two_heads = lambda g: vec(jnp.concatenate([g, g], axis=-1))
    score_bounds = (math.sqrt(HEAD_DIM) * BF16_ROUNDING_SLACK
                    * jnp.max(jnp.abs(q_norm_g), axis=-1) * jnp.max(jnp.abs(k_norm_g), axis=-1))
    params = (meta_tokens, vec(norm1_g), w_in.astype(BF16), two_heads(q_norm_g),
              two_heads(k_norm_g), w_pool.astype(BF16), vec(pool_scale), w_out.astype(BF16),
              vec(norm2_g), w_mlp_in.astype(BF16), w_mlp_out.astype(BF16), bd)
    tables = _rope_tables(max(x_prompt.shape[1], x_sample.shape[1]))
    return (_trunk(x_prompt, tables, score_bounds, *params),
            _trunk(x_sample, tables, score_bounds, *params))
```

```python
import functools
import math
from typing import NamedTuple

import jax
import jax.numpy as jnp
from jax import lax
from jax.experimental import pallas as pl
from jax.experimental.pallas import tpu as pltpu

D_MODEL = 1024
N_META = 16
GRID_W = 64
POOL_WIDTH = 512
POOL_WINDOWS = (2, 4, 8, 16)
POOL_GROUP = 128
HEAD_DIM = 64
N_HEADS = 8
N_KV_HEADS = 2
Q_PER_KV = N_HEADS // N_KV_HEADS
ATTN_WIDTH = N_HEADS * HEAD_DIM
KV_WIDTH = N_KV_HEADS * HEAD_DIM
IN_WIDTH = POOL_WIDTH + ATTN_WIDTH + 2 * KV_WIDTH
ROT_PAIRS = HEAD_DIM // 4
ROPE_THETA = 10000.0
D_FF = 4 * D_MODEL
EPS = 1e-6

LANES = 128
SUBLANES = 8
TAIL = LANES
HALO = SUBLANES
BF16_TILE_ROWS = 16
V_ROWS = HEAD_DIM + BF16_TILE_ROWS
VMEM_LIMIT = 56 * 1024 * 1024
MAX_UNSHIFTED_SCORE = 40.0
BF16_ROUNDING_SLACK = 1.02
SCORES_PER_STEP = 1 << 25

F32 = jnp.float32
BF16 = jnp.bfloat16


def _pick_block(n, cap):
    best = LANES
    for t in range(LANES, cap + 1, LANES):
        if n % t == 0:
            best = t
    return best


class _Plan(NamedTuple):
    t_rows: int
    tq: int
    tk: int
    sub: int
    n_sub: int


def _plan(n_tok):
    sub = min(512, n_tok)
    n_sub = min(16, n_tok // sub)
    assert n_tok % (sub * n_sub) == 0, n_tok
    tq = _pick_block(n_tok, min(1024, SCORES_PER_STEP // (Q_PER_KV * n_tok)))
    return _Plan(_pick_block(n_tok, 1024), tq, _pick_block(n_tok, 512), sub, n_sub)


def _const_spec(shape):
    nd = len(shape)
    return pl.BlockSpec(shape, lambda *_: (0,) * nd, pipeline_mode=pl.Buffered(1))


def _layer_spec(stacked, layer):
    nd = stacked.ndim
    return pl.BlockSpec((None,) + stacked.shape[1:], lambda *_: (layer,) + (0,) * (nd - 1),
                        pipeline_mode=pl.Buffered(1))


def _inproj_rows(rows, h_ref, g1_ref, win_ref, qg_ref, kg_ref, bd_ref, cos_ref, sa_ref, sb_ref,
                 u_ref, qt_ref, k_ref, vt_ref):
    x = h_ref[rows, :]
    ms = jnp.mean(x * x, axis=-1, keepdims=True)
    n = (x * lax.rsqrt(ms + EPS) * g1_ref[...]).astype(BF16)
    proj = jnp.dot(n, win_ref[...], preferred_element_type=F32)
    u_ref[rows, :] = proj[:, :POOL_WIDTH]

    cos, sa, sb, bd = cos_ref[rows, :], sa_ref[rows, :], sb_ref[rows, :], bd_ref[...]

    def norm_rope(xc, g):
        sq = xc * xc
        hi = sq.astype(BF16)
        lo = (sq - hi.astype(F32)).astype(BF16)
        ss = (jnp.dot(hi, bd, preferred_element_type=F32)
              + jnp.dot(lo, bd, preferred_element_type=F32))
        y = xc * lax.rsqrt(ss * (1.0 / HEAD_DIM) + EPS) * g
        return (y * cos + pltpu.roll(y, LANES - ROT_PAIRS, 1) * sa
                + pltpu.roll(y, ROT_PAIRS, 1) * sb)

    scale = 1.0 / math.sqrt(HEAD_DIM)
    for c in range(ATTN_WIDTH // LANES):
        lo_col = POOL_WIDTH + c * LANES
        qc = norm_rope(proj[:, lo_col:lo_col + LANES], qg_ref[...]) * scale
        qt_ref[c, :, rows] = qc.T.astype(BF16)

    k0 = POOL_WIDTH + ATTN_WIDTH
    kc = norm_rope(proj[:, k0:k0 + KV_WIDTH], kg_ref[...])
    k_ref[rows, :] = kc.astype(BF16)

    vt = proj[:, k0 + KV_WIDTH:].T
    extra = V_ROWS - HEAD_DIM
    ones_row = jnp.where(lax.broadcasted_iota(jnp.int32, (extra, vt.shape[1]), 0) == 0, 1.0, 0.0)
    for g in range(N_KV_HEADS):
        vt_ref[g, :, rows] = jnp.concatenate(
            [vt[g * HEAD_DIM:(g + 1) * HEAD_DIM], ones_row], axis=0).astype(BF16)


def _inproj_kernel(h_ref, *refs):
    t = h_ref.shape[0]
    n_part = 2 if t % (2 * LANES) == 0 else 1
    for pi in range(n_part):
        _inproj_rows(slice(pi * (t // n_part), (pi + 1) * (t // n_part)), h_ref, *refs)


def _inproj(h, layer, g1, w_in, qg, kg, bd, cos, sa, sb, t_rows):
    b, rows, d = h.shape
    row_spec = lambda w: pl.BlockSpec((None, t_rows, w), lambda bi, i: (bi, i, 0))
    tab_spec = pl.BlockSpec((t_rows, LANES), lambda bi, i: (i, 0))
    return pl.pallas_call(
        _inproj_kernel,
        grid=(b, rows // t_rows),
        in_specs=[
            row_spec(d),
            _layer_spec(g1, layer), _layer_spec(w_in, layer), _layer_spec(qg, layer),
            _layer_spec(kg, layer),
            _const_spec((LANES, LANES)),
            tab_spec, tab_spec, tab_spec,
        ],
        out_specs=[
            row_spec(POOL_WIDTH),
            pl.BlockSpec((None, ATTN_WIDTH // LANES, LANES, t_rows), lambda bi, i: (bi, 0, 0, i)),
            row_spec(KV_WIDTH),
            pl.BlockSpec((None, N_KV_HEADS, V_ROWS, t_rows), lambda bi, i: (bi, 0, 0, i)),
        ],
        out_shape=[
            jax.ShapeDtypeStruct((b, rows, POOL_WIDTH), F32),
            jax.ShapeDtypeStruct((b, ATTN_WIDTH // LANES, LANES, rows), BF16),
            jax.ShapeDtypeStruct((b, rows, KV_WIDTH), BF16),
            jax.ShapeDtypeStruct((b, N_KV_HEADS, V_ROWS, rows), BF16),
        ],
        compiler_params=pltpu.CompilerParams(
            dimension_semantics=("arbitrary", "arbitrary"), vmem_limit_bytes=VMEM_LIMIT),
        name="inproj",
    )(h, g1, w_in, qg, kg, bd, cos, sa, sb)


def _stack_queries(qs_ref, qt_ref, tq):
    g = pl.program_id(1)
    qs_ref[...] = jnp.zeros_like(qs_ref)
    row0 = pl.multiple_of(g * HEAD_DIM, HEAD_DIM)
    for i in range(Q_PER_KV):
        half = slice((i % 2) * HEAD_DIM, (i % 2 + 1) * HEAD_DIM)
        qs_ref[pl.ds(row0, HEAD_DIM), i * tq:(i + 1) * tq] = qt_ref[i // 2, half, :]
    return qs_ref[...]


def _write_heads(o_ref, acc, tq):
    o = acc[:HEAD_DIM] * (1.0 / acc[HEAD_DIM:HEAD_DIM + 1])
    for a in range(Q_PER_KV // 2):
        pair = jnp.concatenate([o[:, (2 * a) * tq:(2 * a + 1) * tq],
                                o[:, (2 * a + 1) * tq:(2 * a + 2) * tq]], axis=0)
        o_ref[:, a * LANES:(a + 1) * LANES] = pair.T.astype(BF16)


def _attn_online_kernel(qt_ref, k_ref, ktl_ref, vt_ref, vtl_ref, o_ref, qs_ref, *, tk):
    tq = qt_ref.shape[2]
    nk = k_ref.shape[0] // tk
    qs = _stack_queries(qs_ref, qt_ref, tq)

    st = jnp.dot(ktl_ref[:N_META, :], qs, preferred_element_type=F32)
    m = jnp.max(st, axis=0, keepdims=True)
    p = jnp.exp(st - m).astype(BF16)
    acc = jnp.dot(vtl_ref[:, :N_META], p, preferred_element_type=F32)

    def body(j, carry):
        m, acc = carry
        start = pl.multiple_of(j * tk, tk)
        st = jnp.dot(k_ref[pl.ds(start, tk), :], qs, preferred_element_type=F32)
        m_new = jnp.maximum(m, jnp.max(st, axis=0, keepdims=True))
        alpha = jnp.exp(m - m_new)
        p = jnp.exp(st - m_new).astype(BF16)
        acc = alpha * acc + jnp.dot(vt_ref[:, pl.ds(start, tk)], p, preferred_element_type=F32)
        return m_new, acc

    m, acc = lax.fori_loop(0, nk, body, (m, acc))
    _write_heads(o_ref, acc, tq)


def _attn_bounded_kernel(qt_ref, k_ref, ktl_ref, vt_ref, vtl_ref, o_ref, qs_ref, *, sub, n_sub):
    tq = qt_ref.shape[2]
    chunk = sub * n_sub
    n_chunks = k_ref.shape[0] // chunk
    qs = _stack_queries(qs_ref, qt_ref, tq)

    st = jnp.dot(ktl_ref[:N_META, :], qs, preferred_element_type=F32)
    acc = jnp.dot(vtl_ref[:, :N_META], jnp.exp(st).astype(BF16), preferred_element_type=F32)

    def body(j, acc):
        base = j * chunk
        for si in range(n_sub):
            start = pl.multiple_of(base + si * sub, LANES)
            st = jnp.dot(k_ref[pl.ds(start, sub), :], qs, preferred_element_type=F32)
            p = jnp.exp(st).astype(BF16)
            acc = acc + jnp.dot(vt_ref[:, pl.ds(start, sub)], p, preferred_element_type=F32)
        return acc

    acc = lax.fori_loop(0, n_chunks, body, acc)
    _write_heads(o_ref, acc, tq)


def _attention(qt, k, ktl, vt, vtl, tq, plan, bounded):
    b, _, _, rows = qt.shape
    n_tok = k.shape[1]
    gw = Q_PER_KV * HEAD_DIM
    if bounded:
        body = functools.partial(_attn_bounded_kernel, sub=plan.sub, n_sub=plan.n_sub)
    else:
        body = functools.partial(_attn_online_kernel, tk=plan.tk)
    return pl.pallas_call(
        body,
        grid=(b, N_KV_HEADS, rows // tq),
        in_specs=[
            pl.BlockSpec((None, Q_PER_KV // 2, LANES, tq), lambda bi, g, i: (bi, g, 0, i)),
            pl.BlockSpec((None, n_tok, KV_WIDTH), lambda bi, g, i: (bi, 0, 0)),
            pl.BlockSpec((None, TAIL, KV_WIDTH), lambda bi, g, i: (bi, 0, 0)),
            pl.BlockSpec((None, None, V_ROWS, n_tok), lambda bi, g, i: (bi, g, 0, 0)),
            pl.BlockSpec((None, None, V_ROWS, TAIL), lambda bi, g, i: (bi, g, 0, 0)),
        ],
        out_specs=pl.BlockSpec((None, tq, gw), lambda bi, g, i: (bi, i, g)),
        out_shape=jax.ShapeDtypeStruct((b, rows, ATTN_WIDTH), BF16),
        scratch_shapes=[pltpu.VMEM((KV_WIDTH, Q_PER_KV * tq), BF16)],
        compiler_params=pltpu.CompilerParams(
            dimension_semantics=("arbitrary", "arbitrary", "arbitrary"),
            vmem_limit_bytes=VMEM_LIMIT),
        name="attn_bounded" if bounded else "attn_online",
    )(qt, k, ktl, vt, vtl)


def _post_kernel(h_ref, u_ref, upa_ref, upb_ref, un_ref, a_ref, wpool_ref, pscale_ref, wo_ref,
                 g2_ref, w1_ref, w2_ref, o_ref, uext_ref, *, pos_base, seq_len, n_out, ff_chunk):
    t = h_ref.shape[0]
    step = pl.program_id(1)
    pos0 = pos_base + step * t
    r_main = lax.broadcasted_iota(jnp.int32, (t, 1), 0)
    r_halo = lax.broadcasted_iota(jnp.int32, (HALO, 1), 0)
    pos = pos0 + r_main

    u_main = u_ref[...]
    u_prev = jnp.where(step == 0, upb_ref[...], upa_ref[...])
    uext_ref[0:HALO] = jnp.where(pos0 - HALO + r_halo >= 0, u_prev, 0.0)
    uext_ref[HALO:HALO + t] = u_main
    uext_ref[HALO + t:] = jnp.where(pos0 + t + r_halo < seq_len, un_ref[...], 0.0)

    pooled = []
    for gi, w in enumerate(POOL_WINDOWS):
        half = w // 2
        lanes = slice(gi * POOL_GROUP, (gi + 1) * POOL_GROUP)
        base = HALO - half
        wsum = uext_ref[base:base + t, lanes]
        for jj in range(1, w):
            wsum = wsum + uext_ref[base + jj:base + jj + t, lanes]
        cnt = (jnp.minimum(pos + (w - half), seq_len) - jnp.maximum(pos - half, 0)).astype(F32)
        dlt = wsum / cnt - u_main[:, lanes]
        y = jnp.dot(dlt.astype(BF16), wpool_ref[gi], preferred_element_type=F32)
        pooled.append((y * pscale_ref[:, lanes]).astype(BF16))

    mix = jnp.concatenate(pooled + [a_ref[...]], axis=1)
    h1 = h_ref[...] + jnp.dot(mix, wo_ref[...], preferred_element_type=F32)

    ms = jnp.mean(h1 * h1, axis=-1, keepdims=True)
    n2 = (h1 * lax.rsqrt(ms + EPS) * g2_ref[...]).astype(BF16)
    acc = h1
    for c in range(D_FF // ff_chunk):
        cols = slice(c * ff_chunk, (c + 1) * ff_chunk)
        mid = jnp.dot(n2, w1_ref[:, cols], preferred_element_type=F32)
        act = jnp.square(jnp.maximum(mid, 0.0)).astype(BF16)
        acc = acc + jnp.dot(act, w2_ref[cols, :], preferred_element_type=F32)
    if n_out < t:
        acc = jnp.where(r_main < n_out, acc, 0.0)
    o_ref[...] = acc


def _post(h, u, u_prev_first, u_next, next_block, attn, layer, w_pool, pool_scale, w_out, g2,
          w1, w2, t_rows, pos_base, seq_len, n_out):
    b, rows, d = h.shape
    per = t_rows // HALO
    last_halo = u_next.shape[1] // HALO - 1
    row_spec = lambda w: pl.BlockSpec((None, t_rows, w), lambda bi, i: (bi, i, 0))
    halo_spec = lambda fn: pl.BlockSpec((None, HALO, POOL_WIDTH), fn)
    if next_block is None:
        next_fn = lambda bi, i: (bi, jnp.minimum((i + 1) * per, last_halo), 0)
    else:
        next_fn = lambda bi, i: (bi, next_block, 0)
    return pl.pallas_call(
        functools.partial(_post_kernel, pos_base=pos_base, seq_len=seq_len, n_out=n_out,
                          ff_chunk=512),
        grid=(b, rows // t_rows),
        in_specs=[
            row_spec(d), row_spec(POOL_WIDTH),
            halo_spec(lambda bi, i: (bi, jnp.maximum(i * per - 1, 0), 0)),
            halo_spec(lambda bi, i: (bi, 1, 0)),
            halo_spec(next_fn),
            row_spec(ATTN_WIDTH),
            _layer_spec(w_pool, layer), _layer_spec(pool_scale, layer), _layer_spec(w_out, layer),
            _layer_spec(g2, layer), _layer_spec(w1, layer), _layer_spec(w2, layer),
        ],
        out_specs=row_spec(d),
        out_shape=jax.ShapeDtypeStruct((b, rows, d), F32),
        scratch_shapes=[pltpu.VMEM((t_rows + 2 * HALO, POOL_WIDTH), F32)],
        compiler_params=pltpu.CompilerParams(
            dimension_semantics=("arbitrary", "arbitrary"), vmem_limit_bytes=VMEM_LIMIT),
        name="post",
    )(h, u, u, u_prev_first, u_next, attn, w_pool, pool_scale, w_out, g2, w1, w2)


def _rope_tables(n_tok):
    rows = n_tok // GRID_W
    lane = jnp.arange(LANES, dtype=jnp.int32) % HEAD_DIM
    by_row = (lane // (2 * ROT_PAIRS) == 0)[None, None, :]
    first_half = ((lane % (2 * ROT_PAIRS)) // ROT_PAIRS == 0)[None, None, :]
    inv = (ROPE_THETA ** (-jnp.arange(ROT_PAIRS, dtype=F32) / ROT_PAIRS))[lane % ROT_PAIRS]
    ang_r = jnp.arange(rows, dtype=jnp.int32).astype(F32)[:, None] * inv
    ang_c = jnp.arange(GRID_W, dtype=jnp.int32).astype(F32)[:, None] * inv
    pick = lambda fr, fc: jnp.where(by_row, fr[:, None, :], fc[None, :, :])
    cos = pick(jnp.cos(ang_r), jnp.cos(ang_c))
    sin = pick(jnp.sin(ang_r), jnp.sin(ang_c))
    sa = jnp.where(first_half, -sin, 0.0)
    sb = jnp.where(first_half, 0.0, sin)
    return tuple(t.reshape(n_tok, LANES) for t in (cos, sa, sb))


def _trunk(x, tables, score_bounds, meta_tokens, norm1_g, w_in, q_gain, k_gain, w_pool,
           pool_scale, w_out, norm2_g, w_mlp_in, w_mlp_out, bd):
    b, n_tok, d = x.shape
    depth = w_in.shape[0]
    seq_len = N_META + n_tok
    plan = _plan(n_tok)
    cos, sa, sb = tables
    tail_cos, tail_sin = jnp.ones((TAIL, LANES), F32), jnp.zeros((TAIL, LANES), F32)
    meta = jnp.broadcast_to(meta_tokens[None].astype(x.dtype), (b, N_META, d))
    h_main = x
    h_tail = jnp.concatenate([meta, jnp.zeros((b, TAIL - N_META, d), x.dtype)], axis=1)
    for l in range(depth):
        last = l == depth - 1
        shared = (l, norm1_g, w_in, q_gain, k_gain, bd)
        u, qt, k, vt = _inproj(h_main, *shared, cos, sa, sb, plan.t_rows)
        ut, qtl, ktl, vtl = _inproj(h_tail, *shared, tail_cos, tail_sin, tail_sin, TAIL)

        def attend(bounded, qt, qtl, k, ktl, vt, vtl, last=last):
            a_main = _attention(qt, k, ktl, vt, vtl, plan.tq, plan, bounded)
            if last:
                return (a_main,)
            return a_main, _attention(qtl, k, ktl, vt, vtl, TAIL, plan, bounded)

        attn = lax.cond(score_bounds[l] <= MAX_UNSHIFTED_SCORE,
                        functools.partial(attend, True), functools.partial(attend, False),
                        qt, qtl, k, ktl, vt, vtl)

        weights = (l, w_pool, pool_scale, w_out, norm2_g, w_mlp_in, w_mlp_out)
        if not last:
            u_prefix = jnp.concatenate([ut[:, :N_META], u[:, :TAIL - N_META]], axis=1)
            h_tail = _post(h_tail, u_prefix, u_prefix, u, (TAIL - N_META) // HALO, attn[1],
                           *weights, TAIL, 0, seq_len, N_META)
        h_main = _post(h_main, u, ut, u, None, attn[0], *weights, plan.t_rows, N_META, seq_len,
                       plan.t_rows)
    return h_main


def kernel(x_prompt, x_sample, meta_tokens, norm1_g, w_in, q_norm_g, k_norm_g, w_pool,
           pool_scale, w_out, norm2_g, w_mlp_in, w_mlp_out):
    depth = w_in.shape[0]
    head = jnp.arange(LANES, dtype=jnp.int32) // HEAD_DIM
    bd = (head[:, None] == head[None, :]).astype(BF16)
    vec = lambda p: p.reshape(depth, 1, -1)
    two_heads = lambda g: vec(jnp.concatenate([g, g], axis=-1))
    score_bounds = (math.sqrt(HEAD_DIM) * BF16_ROUNDING_SLACK
                    * jnp.max(jnp.abs(q_norm_g), axis=-1) * jnp.max(jnp.abs(k_norm_g), axis=-1))
    params = (meta_tokens, vec(norm1_g), w_in.astype(BF16), two_heads(q_norm_g),
              two_heads(k_norm_g), w_pool.astype(BF16), vec(pool_scale), w_out.astype(BF16),
              vec(norm2_g), w_mlp_in.astype(BF16), w_mlp_out.astype(BF16), bd)
    tables = _rope_tables(max(x_prompt.shape[1], x_sample.shape[1]))
    return (_trunk(x_prompt, tables, score_bounds, *params),
            _trunk(x_sample, tables, score_bounds, *params))
```

```python
import functools
import math
from typing import NamedTuple

import jax
import jax.numpy as jnp
from jax import lax
from jax.experimental import pallas as pl
from jax.experimental.pallas import tpu as pltpu

D_MODEL = 1024
N_META = 16
GRID_W = 64
POOL_WIDTH = 512
POOL_WINDOWS = (2, 4, 8, 16)
POOL_GROUP = 128
HEAD_DIM = 64
N_HEADS = 8
N_KV_HEADS = 2
Q_PER_KV = N_HEADS // N_KV_HEADS
ATTN_WIDTH = N_HEADS * HEAD_DIM
KV_WIDTH = N_KV_HEADS * HEAD_DIM
IN_WIDTH = POOL_WIDTH + ATTN_WIDTH + 2 * KV_WIDTH
ROT_PAIRS = HEAD_DIM // 4
ROPE_THETA = 10000.0
D_FF = 4 * D_MODEL
EPS = 1e-6

LANES = 128
SUBLANES = 8
TAIL = LANES
HALO = SUBLANES
BF16_TILE_ROWS = 16
V_ROWS = HEAD_DIM + BF16_TILE_ROWS
VMEM_LIMIT = 56 * 1024 * 1024
MAX_UNSHIFTED_SCORE = 40.0
BF16_ROUNDING_SLACK = 1.02
SCORES_PER_STEP = 1 << 24

F32 = jnp.float32
BF16 = jnp.bfloat16


def _pick_block(n, cap):
    best = LANES
    for t in range(LANES, cap + 1, LANES):
        if n % t == 0:
            best = t
    return best


class _Plan(NamedTuple):
    t_rows: int
    tq: int
    tk: int
    sub: int
    n_sub: int


def _plan(n_tok):
    sub = min(1024, n_tok)
    n_sub = min(8, n_tok // sub)
    assert n_tok % (sub * n_sub) == 0, n_tok
    tq = _pick_block(n_tok, min(1024, SCORES_PER_STEP // (Q_PER_KV * n_tok)))
    return _Plan(_pick_block(n_tok, 1024), tq, _pick_block(n_tok, 512), sub, n_sub)


def _const_spec(shape):
    nd = len(shape)
    return pl.BlockSpec(shape, lambda *_: (0,) * nd, pipeline_mode=pl.Buffered(1))


def _layer_spec(stacked, layer):
    nd = stacked.ndim
    return pl.BlockSpec((None,) + stacked.shape[1:], lambda *_: (layer,) + (0,) * (nd - 1),
                        pipeline_mode=pl.Buffered(1))


def _inproj_rows(rows, h_ref, g1_ref, win_ref, qg_ref, kg_ref, bd_ref, cos_ref, sa_ref, sb_ref,
                 u_ref, qt_ref, k_ref, vt_ref):
    x = h_ref[rows, :]
    ms = jnp.mean(x * x, axis=-1, keepdims=True)
    n = (x * lax.rsqrt(ms + EPS) * g1_ref[...]).astype(BF16)
    proj = jnp.dot(n, win_ref[...], preferred_element_type=F32)
    u_ref[rows, :] = proj[:, :POOL_WIDTH]

    cos, sa, sb, bd = cos_ref[rows, :], sa_ref[rows, :], sb_ref[rows, :], bd_ref[...]

    def norm_rope(xc, g):
        sq = xc * xc
        hi = sq.astype(BF16)
        lo = (sq - hi.astype(F32)).astype(BF16)
        ss = (jnp.dot(hi, bd, preferred_element_type=F32)
              + jnp.dot(lo, bd, preferred_element_type=F32))
        y = xc * lax.rsqrt(ss * (1.0 / HEAD_DIM) + EPS) * g
        return (y * cos + pltpu.roll(y, LANES - ROT_PAIRS, 1) * sa
                + pltpu.roll(y, ROT_PAIRS, 1) * sb)

    scale = 1.0 / math.sqrt(HEAD_DIM)
    for c in range(ATTN_WIDTH // LANES):
        lo_col = POOL_WIDTH + c * LANES
        qc = norm_rope(proj[:, lo_col:lo_col + LANES], qg_ref[...]) * scale
        qt_ref[c, :, rows] = qc.T.astype(BF16)

    k0 = POOL_WIDTH + ATTN_WIDTH
    kc = norm_rope(proj[:, k0:k0 + KV_WIDTH], kg_ref[...])
    k_ref[rows, :] = kc.astype(BF16)

    vt = proj[:, k0 + KV_WIDTH:].T
    extra = V_ROWS - HEAD_DIM
    ones_row = jnp.where(lax.broadcasted_iota(jnp.int32, (extra, vt.shape[1]), 0) == 0, 1.0, 0.0)
    for g in range(N_KV_HEADS):
        vt_ref[g, :, rows] = jnp.concatenate(
            [vt[g * HEAD_DIM:(g + 1) * HEAD_DIM], ones_row], axis=0).astype(BF16)


def _inproj_kernel(h_ref, *refs):
    t = h_ref.shape[0]
    n_part = 2 if t % (2 * LANES) == 0 else 1
    for pi in range(n_part):
        _inproj_rows(slice(pi * (t // n_part), (pi + 1) * (t // n_part)), h_ref, *refs)


def _inproj(h, layer, g1, w_in, qg, kg, bd, cos, sa, sb, t_rows):
    b, rows, d = h.shape
    row_spec = lambda w: pl.BlockSpec((None, t_rows, w), lambda bi, i: (bi, i, 0))
    tab_spec = pl.BlockSpec((t_rows, LANES), lambda bi, i: (i, 0))
    return pl.pallas_call(
        _inproj_kernel,
        grid=(b, rows // t_rows),
        in_specs=[
            row_spec(d),
            _layer_spec(g1, layer), _layer_spec(w_in, layer), _layer_spec(qg, layer),
            _layer_spec(kg, layer),
            _const_spec((LANES, LANES)),
            tab_spec, tab_spec, tab_spec,
        ],
        out_specs=[
            row_spec(POOL_WIDTH),
            pl.BlockSpec((None, ATTN_WIDTH // LANES, LANES, t_rows), lambda bi, i: (bi, 0, 0, i)),
            row_spec(KV_WIDTH),
            pl.BlockSpec((None, N_KV_HEADS, V_ROWS, t_rows), lambda bi, i: (bi, 0, 0, i)),
        ],
        out_shape=[
            jax.ShapeDtypeStruct((b, rows, POOL_WIDTH), F32),
            jax.ShapeDtypeStruct((b, ATTN_WIDTH // LANES, LANES, rows), BF16),
            jax.ShapeDtypeStruct((b, rows, KV_WIDTH), BF16),
            jax.ShapeDtypeStruct((b, N_KV_HEADS, V_ROWS, rows), BF16),
        ],
        compiler_params=pltpu.CompilerParams(
            dimension_semantics=("arbitrary", "arbitrary"), vmem_limit_bytes=VMEM_LIMIT),
        name="inproj",
    )(h, g1, w_in, qg, kg, bd, cos, sa, sb)


def _stack_queries(qs_ref, qt_ref, tq):
    g = pl.program_id(1)
    qs_ref[...] = jnp.zeros_like(qs_ref)
    row0 = pl.multiple_of(g * HEAD_DIM, HEAD_DIM)
    for i in range(Q_PER_KV):
        half = slice((i % 2) * HEAD_DIM, (i % 2 + 1) * HEAD_DIM)
        qs_ref[pl.ds(row0, HEAD_DIM), i * tq:(i + 1) * tq] = qt_ref[i // 2, half, :]
    return qs_ref[...]


def _write_heads(o_ref, acc, tq):
    o = acc[:HEAD_DIM] * (1.0 / acc[HEAD_DIM:HEAD_DIM + 1])
    for a in range(Q_PER_KV // 2):
        pair = jnp.concatenate([o[:, (2 * a) * tq:(2 * a + 1) * tq],
                                o[:, (2 * a + 1) * tq:(2 * a + 2) * tq]], axis=0)
        o_ref[:, a * LANES:(a + 1) * LANES] = pair.T.astype(BF16)


def _attn_online_kernel(qt_ref, k_ref, ktl_ref, vt_ref, vtl_ref, o_ref, qs_ref, *, tk):
    tq = qt_ref.shape[2]
    nk = k_ref.shape[0] // tk
    qs = _stack_queries(qs_ref, qt_ref, tq)

    st = jnp.dot(ktl_ref[:N_META, :], qs, preferred_element_type=F32)
    m = jnp.max(st, axis=0, keepdims=True)
    p = jnp.exp(st - m).astype(BF16)
    acc = jnp.dot(vtl_ref[:, :N_META], p, preferred_element_type=F32)

    def body(j, carry):
        m, acc = carry
        start = pl.multiple_of(j * tk, tk)
        st = jnp.dot(k_ref[pl.ds(start, tk), :], qs, preferred_element_type=F32)
        m_new = jnp.maximum(m, jnp.max(st, axis=0, keepdims=True))
        alpha = jnp.exp(m - m_new)
        p = jnp.exp(st - m_new).astype(BF16)
        acc = alpha * acc + jnp.dot(vt_ref[:, pl.ds(start, tk)], p, preferred_element_type=F32)
        return m_new, acc

    m, acc = lax.fori_loop(0, nk, body, (m, acc))
    _write_heads(o_ref, acc, tq)


def _attn_bounded_kernel(qt_ref, k_ref, ktl_ref, vt_ref, vtl_ref, o_ref, qs_ref, *, sub, n_sub):
    tq = qt_ref.shape[2]
    chunk = sub * n_sub
    n_chunks = k_ref.shape[0] // chunk
    qs = _stack_queries(qs_ref, qt_ref, tq)

    st = jnp.dot(ktl_ref[:N_META, :], qs, preferred_element_type=F32)
    acc = jnp.dot(vtl_ref[:, :N_META], jnp.exp(st).astype(BF16), preferred_element_type=F32)

    def body(j, acc):
        base = j * chunk
        for si in range(n_sub):
            start = pl.multiple_of(base + si * sub, LANES)
            st = jnp.dot(k_ref[pl.ds(start, sub), :], qs, preferred_element_type=F32)
            p = jnp.exp(st).astype(BF16)
            acc = acc + jnp.dot(vt_ref[:, pl.ds(start, sub)], p, preferred_element_type=F32)
        return acc

    acc = lax.fori_loop(0, n_chunks, body, acc)
    _write_heads(o_ref, acc, tq)


def _attention(qt, k, ktl, vt, vtl, tq, plan, bounded):
    b, _, _, rows = qt.shape
    n_tok = k.shape[1]
    gw = Q_PER_KV * HEAD_DIM
    if bounded:
        body = functools.partial(_attn_bounded_kernel, sub=plan.sub, n_sub=plan.n_sub)
    else:
        body = functools.partial(_attn_online_kernel, tk=plan.tk)
    return pl.pallas_call(
        body,
        grid=(b, N_KV_HEADS, rows // tq),
        in_specs=[
            pl.BlockSpec((None, Q_PER_KV // 2, LANES, tq), lambda bi, g, i: (bi, g, 0, i)),
            pl.BlockSpec((None, n_tok, KV_WIDTH), lambda bi, g, i: (bi, 0, 0)),
            pl.BlockSpec((None, TAIL, KV_WIDTH), lambda bi, g, i: (bi, 0, 0)),
            pl.BlockSpec((None, None, V_ROWS, n_tok), lambda bi, g, i: (bi, g, 0, 0)),
            pl.BlockSpec((None, None, V_ROWS, TAIL), lambda bi, g, i: (bi, g, 0, 0)),
        ],
        out_specs=pl.BlockSpec((None, tq, gw), lambda bi, g, i: (bi, i, g)),
        out_shape=jax.ShapeDtypeStruct((b, rows, ATTN_WIDTH), BF16),
        scratch_shapes=[pltpu.VMEM((KV_WIDTH, Q_PER_KV * tq), BF16)],
        compiler_params=pltpu.CompilerParams(
            dimension_semantics=("arbitrary", "arbitrary", "arbitrary"),
            vmem_limit_bytes=VMEM_LIMIT),
        name="attn_bounded" if bounded else "attn_online",
    )(qt, k, ktl, vt, vtl)


def _post_kernel(h_ref, u_ref, upa_ref, upb_ref, un_ref, a_ref, wpool_ref, pscale_ref, wo_ref,
                 g2_ref, w1_ref, w2_ref, o_ref, uext_ref, *, pos_base, seq_len, n_out, ff_chunk):
    t = h_ref.shape[0]
    step = pl.program_id(1)
    pos0 = pos_base + step * t
    r_main = lax.broadcasted_iota(jnp.int32, (t, 1), 0)
    r_halo = lax.broadcasted_iota(jnp.int32, (HALO, 1), 0)
    pos = pos0 + r_main

    u_main = u_ref[...]
    u_prev = jnp.where(step == 0, upb_ref[...], upa_ref[...])
    uext_ref[0:HALO] = jnp.where(pos0 - HALO + r_halo >= 0, u_prev, 0.0)
    uext_ref[HALO:HALO + t] = u_main
    uext_ref[HALO + t:] = jnp.where(pos0 + t + r_halo < seq_len, un_ref[...], 0.0)

    pooled = []
    for gi, w in enumerate(POOL_WINDOWS):
        half = w // 2
        lanes = slice(gi * POOL_GROUP, (gi + 1) * POOL_GROUP)
        base = HALO - half
        wsum = uext_ref[base:base + t, lanes]
        for jj in range(1, w):
            wsum = wsum + uext_ref[base + jj:base + jj + t, lanes]
        cnt = (jnp.minimum(pos + (w - half), seq_len) - jnp.maximum(pos - half, 0)).astype(F32)
        dlt = wsum / cnt - u_main[:, lanes]
        y = jnp.dot(dlt.astype(BF16), wpool_ref[gi], preferred_element_type=F32)
        pooled.append((y * pscale_ref[:, lanes]).astype(BF16))

    mix = jnp.concatenate(pooled + [a_ref[...]], axis=1)
    h1 = h_ref[...] + jnp.dot(mix, wo_ref[...], preferred_element_type=F32)

    ms = jnp.mean(h1 * h1, axis=-1, keepdims=True)
    n2 = (h1 * lax.rsqrt(ms + EPS) * g2_ref[...]).astype(BF16)
    acc = h1
    for c in range(D_FF // ff_chunk):
        cols = slice(c * ff_chunk, (c + 1) * ff_chunk)
        mid = jnp.dot(n2, w1_ref[:, cols], preferred_element_type=F32)
        act = jnp.square(jnp.maximum(mid, 0.0)).astype(BF16)
        acc = acc + jnp.dot(act, w2_ref[cols, :], preferred_element_type=F32)
    if n_out < t:
        acc = jnp.where(r_main < n_out, acc, 0.0)
    o_ref[...] = acc


def _post(h, u, u_prev_first, u_next, next_block, attn, layer, w_pool, pool_scale, w_out, g2,
          w1, w2, t_rows, pos_base, seq_len, n_out):
    b, rows, d = h.shape
    per = t_rows // HALO
    last_halo = u_next.shape[1] // HALO - 1
    row_spec = lambda w: pl.BlockSpec((None, t_rows, w), lambda bi, i: (bi, i, 0))
    halo_spec = lambda fn: pl.BlockSpec((None, HALO, POOL_WIDTH), fn)
    if next_block is None:
        next_fn = lambda bi, i: (bi, jnp.minimum((i + 1) * per, last_halo), 0)
    else:
        next_fn = lambda bi, i: (bi, next_block, 0)
    return pl.pallas_call(
        functools.partial(_post_kernel, pos_base=pos_base, seq_len=seq_len, n_out=n_out,
                          ff_chunk=512),
        grid=(b, rows // t_rows),
        in_specs=[
            row_spec(d), row_spec(POOL_WIDTH),
            halo_spec(lambda bi, i: (bi, jnp.maximum(i * per - 1, 0), 0)),
            halo_spec(lambda bi, i: (bi, 1, 0)),
            halo_spec(next_fn),
            row_spec(ATTN_WIDTH),
            _layer_spec(w_pool, layer), _layer_spec(pool_scale, layer), _layer_spec(w_out, layer),
            _layer_spec(g2, layer), _layer_spec(w1, layer), _layer_spec(w2, layer),
        ],
        out_specs=row_spec(d),
        out_shape=jax.ShapeDtypeStruct((b, rows, d), F32),
        scratch_shapes=[pltpu.VMEM((t_rows + 2 * HALO, POOL_WIDTH), F32)],
        compiler_params=pltpu.CompilerParams(
            dimension_semantics=("arbitrary", "arbitrary"), vmem_limit_bytes=VMEM_LIMIT),
        name="post",
    )(h, u, u, u_prev_first, u_next, attn, w_pool, pool_scale, w_out, g2, w1, w2)


def _rope_tables(n_tok):
    rows = n_tok // GRID_W
    lane = jnp.arange(LANES, dtype=jnp.int32) % HEAD_DIM
    by_row = (lane // (2 * ROT_PAIRS) == 0)[None, None, :]
    first_half = ((lane % (2 * ROT_PAIRS)) // ROT_PAIRS == 0)[None, None, :]
    inv = (ROPE_THETA ** (-jnp.arange(ROT_PAIRS, dtype=F32) / ROT_PAIRS))[lane % ROT_PAIRS]
    ang_r = jnp.arange(rows, dtype=jnp.int32).astype(F32)[:, None] * inv
    ang_c = jnp.arange(GRID_W, dtype=jnp.int32).astype(F32)[:, None] * inv
    pick = lambda fr, fc: jnp.where(by_row, fr[:, None, :], fc[None, :, :])
    cos = pick(jnp.cos(ang_r), jnp.cos(ang_c))
    sin = pick(jnp.sin(ang_r), jnp.sin(ang_c))
    sa = jnp.where(first_half, -sin, 0.0)
    sb = jnp.where(first_half, 0.0, sin)
    return tuple(t.reshape(n_tok, LANES) for t in (cos, sa, sb))


def _trunk(x, tables, score_bounds, meta_tokens, norm1_g, w_in, q_gain, k_gain, w_pool,
           pool_scale, w_out, norm2_g, w_mlp_in, w_mlp_out, bd):
    b, n_tok, d = x.shape
    depth = w_in.shape[0]
    seq_len = N_META + n_tok
    plan = _plan(n_tok)
    cos, sa, sb = tables
    tail_cos, tail_sin = jnp.ones((TAIL, LANES), F32), jnp.zeros((TAIL, LANES), F32)
    meta = jnp.broadcast_to(meta_tokens[None].astype(x.dtype), (b, N_META, d))
    h_main = x
    h_tail = jnp.concatenate([meta, jnp.zeros((b, TAIL - N_META, d), x.dtype)], axis=1)
    for l in range(depth):
        last = l == depth - 1
        shared = (l, norm1_g, w_in, q_gain, k_gain, bd)
        u, qt, k, vt = _inproj(h_main, *shared, cos, sa, sb, plan.t_rows)
        ut, qtl, ktl, vtl = _inproj(h_tail, *shared, tail_cos, tail_sin, tail_sin, TAIL)

        def attend(bounded, qt, qtl, k, ktl, vt, vtl, last=last):
            a_main = _attention(qt, k, ktl, vt, vtl, plan.tq, plan, bounded)
            if last:
                return (a_main,)
            return a_main, _attention(qtl, k, ktl, vt, vtl, TAIL, plan, bounded)

        attn = lax.cond(score_bounds[l] <= MAX_UNSHIFTED_SCORE,
                        functools.partial(attend, True), functools.partial(attend, False),
                        qt, qtl, k, ktl, vt, vtl)

        weights = (l, w_pool, pool_scale, w_out, norm2_g, w_mlp_in, w_mlp_out)
        if not last:
            u_prefix = jnp.concatenate([ut[:, :N_META], u[:, :TAIL - N_META]], axis=1)
            h_tail = _post(h_tail, u_prefix, u_prefix, u, (TAIL - N_META) // HALO, attn[1],
                           *weights, TAIL, 0, seq_len, N_META)
        h_main = _post(h_main, u, ut, u, None, attn[0], *weights, plan.t_rows, N_META, seq_len,
                       plan.t_rows)
    return h_main


def kernel(x_prompt, x_sample, meta_tokens, norm1_g, w_in, q_norm_g, k_norm_g, w_pool,
           pool_scale, w_out, norm2_g, w_mlp_in, w_mlp_out):
    depth = w_in.shape[0]
    head = jnp.arange(LANES, dtype=jnp.int32) // HEAD_DIM
    bd = (head[:, None] == head[None, :]).astype(BF16)
    vec = lambda p: p.reshape(depth, 1, -1)
    two_heads = lambda g: vec(jnp.concatenate([g, g], axis=-1))
    score_bounds = (math.sqrt(HEAD_DIM) * BF16_ROUNDING_SLACK
                    * jnp.max(jnp.abs(q_norm_g), axis=-1) * jnp.max(jnp.abs(k_norm_g), axis=-1))
    params = (meta_tokens, vec(norm1_g), w_in.astype(BF16), two_heads(q_norm_g),
              two_heads(k_norm_g), w_pool.astype(BF16), vec(pool_scale), w_out.astype(BF16),
              vec(norm2_g), w_mlp_in.astype(BF16), w_mlp_out.astype(BF16), bd)
    tables = _rope_tables(max(x_prompt.shape[1], x_sample.shape[1]))
    return (_trunk(x_prompt, tables, score_bounds, *params),
            _trunk(x_sample, tables, score_bounds, *params))
```
